```python
import math
import jax, jax.numpy as jnp
from jax import lax
import numpy as np

D_MODEL = 1024
BATCH = 8
SEQ = 2048
DEPTH = 4
DEC_BATCH = 128
DEC_SEQ = 1
PAST_LEN = 2048
PAGE_SIZE = 128

N_A_LAYERS = DEPTH // 2
N_B_LAYERS = DEPTH - N_A_LAYERS
E_A = D_MODEL
GROUP_CH = 16
N_GROUPS = E_A // GROUP_CH
P_STATE = 64
N_HEADS = 16
HEAD_DIM = D_MODEL // N_HEADS
E_B = N_HEADS * HEAD_DIM
Q_BLOCK = 128
SB_BIAS_INIT = -7.0
EPS = 1e-6

kernel_name = 'yoco_s5_stickbreak_decoder_step'

F32 = jnp.float32


def rms_norm(x, g):
    xf = x.astype(F32)
    y = xf * lax.rsqrt(jnp.mean(xf * xf, axis=-1, keepdims=True) + EPS)
    return y * g.astype(F32)


def ada_mod(c, w_mod, b_mod, n):
    m = jax.nn.silu(c.astype(F32)) @ w_mod.astype(F32) + b_mod.astype(F32)
    return jnp.split(m[:, None, :], n, axis=-1)


def s5_discretise(lam_re, lam_im, log_dt, bmat_re, bmat_im):
    lam = lax.complex(lam_re.astype(F32), lam_im.astype(F32))
    dt = jnp.exp(log_dt.astype(F32))[:, None]
    a_bar = jnp.exp(lam * dt)
    bmat = lax.complex(bmat_re.astype(F32), bmat_im.astype(F32))
    b_bar = ((a_bar - 1.0) / lam)[..., None] * bmat
    return a_bar, b_bar


def ssm_combine(e1, e2):
    a1, b1 = e1
    a2, b2 = e2
    return a1 * a2, a2 * b1 + b2


def s5_mixer(h, h0_re, h0_im, w_in, lam_re, lam_im, log_dt, bmat_re, bmat_im,
             cmat_re, cmat_im, d_skip, w_glu, b_glu, w_out):
    n_b, t_len, _ = h.shape
    uz = h @ w_in.astype(F32)
    u, z = uz[..., :E_A], uz[..., E_A:]
    a_bar, b_bar = s5_discretise(lam_re, lam_im, log_dt, bmat_re, bmat_im)
    ug = u.reshape(n_b, t_len, N_GROUPS, GROUP_CH)
    bu = lax.complex(jnp.einsum('btgc,gpc->btgp', ug, jnp.real(b_bar)),
                     jnp.einsum('btgc,gpc->btgp', ug, jnp.imag(b_bar)))
    h0 = lax.complex(h0_re.astype(F32), h0_im.astype(F32))
    bu = bu.at[:, 0].add(a_bar[None] * h0)
    a_seq = jnp.broadcast_to(a_bar, (1, t_len, N_GROUPS, P_STATE))
    _, hs = lax.associative_scan(ssm_combine, (a_seq, bu), axis=1)
    cmat = lax.complex(cmat_re.astype(F32), cmat_im.astype(F32))
    y = jnp.real(jnp.einsum('btgp,gcp->btgc', hs, cmat)).reshape(n_b, t_len, E_A)
    y = y + d_skip.astype(F32) * u
    y = jax.nn.gelu(y)
    y = y * jax.nn.sigmoid(y @ w_glu.astype(F32) + b_glu.astype(F32))
    y = y * jax.nn.silu(z)
    return y @ w_out.astype(F32), hs[:, -1]


def shared_kv(x, c, g_kv, w_mod_kv, b_mod_kv, w_kv):
    n_b, t_len, _ = x.shape
    shift, scale = ada_mod(c, w_mod_kv, b_mod_kv, 2)
    h = rms_norm(x, g_kv) * (1.0 + scale) + shift
    kv = h @ w_kv.astype(F32)
    k = kv[..., :E_B].reshape(n_b, t_len, N_HEADS, HEAD_DIM)
    v = kv[..., E_B:].reshape(n_b, t_len, N_HEADS, HEAD_DIM)
    return k, v


def sb_attend(q, k, v, bias, q_pos, k_pos):
    z = jnp.einsum('bqhd,bkhd->bhqk', q.astype(F32), k.astype(F32)) * (HEAD_DIM ** -0.5)
    z = z + bias.astype(F32)[None, :, None, None]
    mask = k_pos[None, :] < q_pos[:, None]
    log_1m = jnp.where(mask, jax.nn.log_sigmoid(-z), 0.0)
    log_stick = lax.cumsum(log_1m, axis=3, reverse=True) - log_1m
    w = jnp.where(mask, jnp.exp(jax.nn.log_sigmoid(z) + log_stick), 0.0)
    o = jnp.einsum('bhqk,bkhd->bqhd', w, v.astype(F32))
    return o.reshape(q.shape[0], q.shape[1], E_B)


def sb_attend_blocked(q, k, v, bias, q_start):
    t_q = q.shape[1]
    outs = []
    for t0 in range(0, t_q, Q_BLOCK):
        t1 = min(t0 + Q_BLOCK, t_q)
        k_end = q_start + t1
        q_pos = q_start + jnp.arange(t0, t1)
        k_pos = jnp.arange(k_end)
        outs.append(sb_attend(q[:, t0:t1], k[:, :k_end], v[:, :k_end], bias, q_pos, k_pos))
    return jnp.concatenate(outs, axis=1)


def sb_mixer(h, keys, vals, q_start, w_in, bias, w_out):
    n_b, t_len, _ = h.shape
    qz = h @ w_in.astype(F32)
    q = qz[..., :E_B].reshape(n_b, t_len, N_HEADS, HEAD_DIM)
    z = qz[..., E_B:]
    o = sb_attend_blocked(q, keys, vals, bias, q_start)
    return (o * jax.nn.silu(z)) @ w_out.astype(F32)


def trunk(x, c, h0_re, h0_im, past_k, past_v, q_start, p):
    x = x.astype(F32)
    ssm_re, ssm_im = [], []
    keys = vals = k_new = v_new = None
    for layer in range(DEPTH):
        if layer < N_A_LAYERS:
            i = layer
            shift, scale, gate = ada_mod(c, p['w_mod_a'][i], p['b_mod_a'][i], 3)
            h = rms_norm(x, p['g_pre_a'][i]) * (1.0 + scale) + shift
            y, h_last = s5_mixer(h, h0_re[i], h0_im[i], p['w_in_a'][i], p['lam_re'][i],
                                 p['lam_im'][i], p['log_dt'][i], p['bmat_re'][i],
                                 p['bmat_im'][i], p['cmat_re'][i], p['cmat_im'][i],
                                 p['d_skip'][i], p['w_glu'][i], p['b_glu'][i],
                                 p['w_out_a'][i])
            x = x + gate * rms_norm(y, p['g_post_a'][i])
            ssm_re.append(jnp.real(h_last))
            ssm_im.append(jnp.imag(h_last))
            if layer == N_A_LAYERS - 1:
                k_new, v_new = shared_kv(x, c, p['g_kv'], p['w_mod_kv'], p['b_mod_kv'], p['w_kv'])
                if past_k is None:
                    keys, vals = k_new, v_new
                else:
                    keys = jnp.concatenate([past_k.astype(F32), k_new], axis=1)
                    vals = jnp.concatenate([past_v.astype(F32), v_new], axis=1)
        else:
            j = layer - N_A_LAYERS
            shift, scale, gate = ada_mod(c, p['w_mod_b'][j], p['b_mod_b'][j], 3)
            h = rms_norm(x, p['g_pre_b'][j]) * (1.0 + scale) + shift
            y = sb_mixer(h, keys, vals, q_start, p['w_in_b'][j], p['sb_bias'][j],
                         p['w_out_b'][j])
            x = x + gate * rms_norm(y, p['g_post_b'][j])
    return x, jnp.stack(ssm_re), jnp.stack(ssm_im), k_new, v_new


def setup_inputs(seed: int = 0) -> dict:
    key = jax.random.key(seed)
    ks = iter(jax.random.split(key, 48))

    def nrm(shape, s):
        return s * jax.random.normal(next(ks), shape, F32)

    n_pages = PAST_LEN // PAGE_SIZE
    n_used = DEC_BATCH * n_pages
    n_phys = n_used + max(1, n_used // 4)
    perm = jax.random.permutation(next(ks), n_phys)
    page_table = perm[:n_used].reshape(DEC_BATCH, n_pages).astype(jnp.int32)

    na, nb, d = N_A_LAYERS, N_B_LAYERS, D_MODEL
    lam_im_base = jnp.broadcast_to(jnp.pi * jnp.arange(P_STATE, dtype=F32), (na, N_GROUPS, P_STATE))
    return {
        'x_prompt': nrm((BATCH, SEQ, d), 1.0),
        'x_sample': nrm((DEC_BATCH, DEC_SEQ, d), 1.0),
        'c_prompt': nrm((BATCH, d), 1.0),
        'c_sample': nrm((DEC_BATCH, d), 1.0),
        'state_ssm_re': nrm((na, DEC_BATCH, N_GROUPS, P_STATE), 0.3),
        'state_ssm_im': nrm((na, DEC_BATCH, N_GROUPS, P_STATE), 0.3),
        'cache_k': nrm((n_phys, PAGE_SIZE, N_HEADS, HEAD_DIM), 1.0),
        'cache_v': nrm((n_phys, PAGE_SIZE, N_HEADS, HEAD_DIM), 1.0),
        'page_table': page_table,
        'g_pre_a': 1.0 + nrm((na, d), 0.05),
        'g_post_a': 1.0 + nrm((na, d), 0.05),
        'w_mod_a': nrm((na, d, 3 * d), 0.5 * d ** -0.5),
        'b_mod_a': nrm((na, 3 * d), 0.02),
        'w_in_a': nrm((na, d, 2 * E_A), d ** -0.5),
        'lam_re': -0.5 + nrm((na, N_GROUPS, P_STATE), 0.01),
        'lam_im': lam_im_base + nrm((na, N_GROUPS, P_STATE), 0.01),
        'log_dt': jax.random.uniform(next(ks), (na, N_GROUPS), F32,
                                     minval=math.log(1e-3), maxval=math.log(1e-1)),
        'bmat_re': nrm((na, N_GROUPS, P_STATE, GROUP_CH), (2 * GROUP_CH) ** -0.5),
        'bmat_im': nrm((na, N_GROUPS, P_STATE, GROUP_CH), (2 * GROUP_CH) ** -0.5),
        'cmat_re': nrm((na, N_GROUPS, GROUP_CH, P_STATE), (2 * P_STATE) ** -0.5),
        'cmat_im': nrm((na, N_GROUPS, GROUP_CH, P_STATE), (2 * P_STATE) ** -0.5),
        'd_skip': nrm((na, E_A), 1.0),
        'w_glu': nrm((na, E_A, E_A), E_A ** -0.5),
        'b_glu': nrm((na, E_A), 0.02),
        'w_out_a': nrm((na, E_A, d), E_A ** -0.5),
        'g_kv': 1.0 + nrm((d,), 0.05),
        'w_mod_kv': nrm((d, 2 * d), 0.5 * d ** -0.5),
        'b_mod_kv': nrm((2 * d,), 0.02),
        'w_kv': nrm((d, 2 * E_B), d ** -0.5),
        'g_pre_b': 1.0 + nrm((nb, d), 0.05),
        'g_post_b': 1.0 + nrm((nb, d), 0.05),
        'w_mod_b': nrm((nb, d, 3 * d), 0.5 * d ** -0.5),
        'b_mod_b': nrm((nb, 3 * d), 0.02),
        'w_in_b': nrm((nb, d, 2 * E_B), d ** -0.5),
        'sb_bias': SB_BIAS_INIT + nrm((nb, N_HEADS), 0.5),
        'w_out_b': nrm((nb, E_B, d), E_B ** -0.5),
    }


def reference(x_prompt, x_sample, c_prompt, c_sample, state_ssm_re, state_ssm_im,
              cache_k, cache_v, page_table, g_pre_a, g_post_a, w_mod_a, b_mod_a, w_in_a,
              lam_re, lam_im, log_dt, bmat_re, bmat_im, cmat_re, cmat_im, d_skip, w_glu,
              b_glu, w_out_a, g_kv, w_mod_kv, b_mod_kv, w_kv, g_pre_b, g_post_b, w_mod_b,
              b_mod_b, w_in_b, sb_bias, w_out_b):
    p = {
        'g_pre_a': g_pre_a, 'g_post_a': g_post_a, 'w_mod_a': w_mod_a, 'b_mod_a': b_mod_a,
        'w_in_a': w_in_a, 'lam_re': lam_re, 'lam_im': lam_im, 'log_dt': log_dt,
        'bmat_re': bmat_re, 'bmat_im': bmat_im, 'cmat_re': cmat_re, 'cmat_im': cmat_im,
        'd_skip': d_skip, 'w_glu': w_glu, 'b_glu': b_glu, 'w_out_a': w_out_a,
        'g_kv': g_kv, 'w_mod_kv': w_mod_kv, 'b_mod_kv': b_mod_kv, 'w_kv': w_kv,
        'g_pre_b': g_pre_b, 'g_post_b': g_post_b, 'w_mod_b': w_mod_b, 'b_mod_b': b_mod_b,
        'w_in_b': w_in_b, 'sb_bias': sb_bias, 'w_out_b': w_out_b,
    }
    n_prompt = x_prompt.shape[0]
    zeros = jnp.zeros((N_A_LAYERS, n_prompt, N_GROUPS, P_STATE), F32)
    y_prompt, p_ssm_re, p_ssm_im, p_k, p_v = trunk(x_prompt, c_prompt, zeros, zeros,
                                                   None, None, 0, p)
    n_seq, n_pg = page_table.shape
    past_len = n_pg * cache_k.shape[1]
    past_k = cache_k[page_table].reshape(n_seq, past_len, N_HEADS, HEAD_DIM)
    past_v = cache_v[page_table].reshape(n_seq, past_len, N_HEADS, HEAD_DIM)
    y_sample, s_ssm_re, s_ssm_im, s_k, s_v = trunk(x_sample, c_sample, state_ssm_re,
                                                   state_ssm_im, past_k, past_v,
                                                   past_len, p)
    return (y_prompt, y_sample, p_ssm_re, p_ssm_im, p_k, p_v, s_ssm_re, s_ssm_im, s_k, s_v)
```

```python
import functools

import jax
import jax.numpy as jnp
from jax import lax
from jax.experimental import pallas as pl
from jax.experimental.pallas import tpu as pltpu

F32 = jnp.float32
BF16 = jnp.bfloat16

D_MODEL = 1024
GROUP_CH = 16
N_GROUPS = D_MODEL // GROUP_CH
P_STATE = 64
N_HEADS = 16
HEAD_DIM = D_MODEL // N_HEADS
EPS = 1e-6

SSM_CHUNK = 16
GROUPS_PER_OCTET = 8
LANES = 128
SUBLANES = 8
ROW_TILE = 512
ATTN_TILE = 128
PAGES_PER_STEP = 4
VMEM_LIMIT = 48 * 1024 * 1024

_NT = (((1,), (1,)), ((), ()))


def _params(*sem):
    return pltpu.CompilerParams(dimension_semantics=sem, vmem_limit_bytes=VMEM_LIMIT)


def _sigmoid(x):
    return 1.0 / (1.0 + jnp.exp(-x))


def _softplus(z):
    return jnp.maximum(z, 0.0) + jnp.log(1.0 + jnp.exp(-jnp.abs(z)))


def _gelu_tanh(x):
    return 0.5 * x * (1.0 + jnp.tanh(0.7978845608028654 * (x + 0.044715 * (x * x * x))))


def _split_bf16(x):
    hi = x.astype(BF16)
    lo = (x - hi.astype(F32)).astype(BF16)
    return hi, lo


def _dot(a, b):
    return jnp.dot(a, b, preferred_element_type=F32)


def _dot3(a, b, dims=None):
    a_hi, a_lo = _split_bf16(a)
    b_hi, b_lo = _split_bf16(b)
    if dims is None:
        f = _dot
    else:
        f = lambda p, q: lax.dot_general(p, q, dims, preferred_element_type=F32)
    return f(a_hi, b_hi) + (f(a_hi, b_lo) + f(a_lo, b_hi))


def _rms(x, g):
    return x * lax.rsqrt(jnp.mean(x * x, axis=-1, keepdims=True) + EPS) * g


def _mod_kernel(c_ref, w_ref, b_ref, o_ref):
    c = c_ref[...]
    s = (c * _sigmoid(c)).astype(BF16)
    o_ref[...] = _dot(s, w_ref[...]) + b_ref[...]


def _ada_mod(c, w, b):
    m, d = c.shape
    n = w.shape[1]
    tn = 1024
    return pl.pallas_call(
        _mod_kernel,
        grid=(n // tn,),
        in_specs=[pl.BlockSpec((m, d), lambda j: (0, 0)),
                  pl.BlockSpec((d, tn), lambda j: (0, j)),
                  pl.BlockSpec((1, tn), lambda j: (0, j))],
        out_specs=pl.BlockSpec((m, tn), lambda j: (0, j)),
        out_shape=jax.ShapeDtypeStruct((m, n), F32),
        compiler_params=_params("parallel"),
        name="ada_mod",
    )(c, w, b.reshape(1, n))


def _norm_mm_kernel(x_ref, sh_ref, sc_ref, g_ref, w_ref, *out_refs, cfg, nchunk):
    h = _rms(x_ref[0], g_ref[...]) * (1.0 + sc_ref[0]) + sh_ref[0]
    hb = h.astype(BF16)
    for col0, ncols, outs in cfg:
        for j in range(0, ncols, nchunk):
            r = _dot(hb, w_ref[:, col0 + j:col0 + j + nchunk])
            for oi, scale in outs:
                v = r if scale == 1.0 else r * scale
                out_refs[oi][0, :, j:j + nchunk] = v.astype(out_refs[oi].dtype)


def _norm_mm(x, m, g, w, cfg, out_dtypes):
    bx, t, d = x.shape
    tm = min(ROW_TILE, t)
    per_row = m.shape[1] != 1
    tmm = tm if per_row else 1
    mod_idx = (lambda col: (lambda b, i: (b, i, col))) if per_row else (lambda col: (lambda b, i: (b, 0, col)))
    widths = {}
    for _, ncols, outs in cfg:
        for oi, _ in outs:
            widths[oi] = ncols
    n_out = len(out_dtypes)
    return pl.pallas_call(
        functools.partial(_norm_mm_kernel, cfg=cfg, nchunk=512),
        grid=(bx, t // tm),
        in_specs=[pl.BlockSpec((1, tm, d), lambda b, i: (b, i, 0)),
                  pl.BlockSpec((1, tmm, d), mod_idx(0)),
                  pl.BlockSpec((1, tmm, d), mod_idx(1)),
                  pl.BlockSpec((1, d), lambda b, i: (0, 0)),
                  pl.BlockSpec(w.shape, lambda b, i: (0, 0))],
        out_specs=[pl.BlockSpec((1, tm, widths[oi]), lambda b, i: (b, i, 0)) for oi in range(n_out)],
        out_shape=[jax.ShapeDtypeStruct((bx, t, widths[oi]), out_dtypes[oi]) for oi in range(n_out)],
        compiler_params=_params("parallel", "parallel"),
        name="norm_proj",
    )(x, m, m, g.reshape(1, d), w)


def _residual_tail(y, w_out_ref, g_ref, gate_ref, x_ref, o_ref):
    o = _dot(y.astype(BF16), w_out_ref[...])
    o_ref[0] = x_ref[0] + gate_ref[0] * _rms(o, g_ref[...])


def _s5_post_kernel(y_ref, u_ref, z_ref, dsk_ref, wglu_ref, bglu_ref, wout_ref, g_ref, gate_ref,
                    x_ref, o_ref):
    y = _gelu_tanh(y_ref[0] + dsk_ref[...] * u_ref[0])
    y = y * _sigmoid(_dot(y.astype(BF16), wglu_ref[...]) + bglu_ref[...])
    z = z_ref[0]
    y = y * (z * _sigmoid(z))
    _residual_tail(y, wout_ref, g_ref, gate_ref, x_ref, o_ref)


def _sb_post_kernel(a_ref, z_ref, wout_ref, g_ref, gate_ref, x_ref, o_ref):
    z = z_ref[0]
    y = a_ref[0] * (z * _sigmoid(z))
    _residual_tail(y, wout_ref, g_ref, gate_ref, x_ref, o_ref)


def _row_specs(x, m):
    bx, t, d = x.shape
    tm = min(ROW_TILE, t)
    per_row = m.shape[1] != 1
    tmm = tm if per_row else 1
    gate_idx = (lambda b, i: (b, i, 2)) if per_row else (lambda b, i: (b, 0, 2))
    row = lambda col: pl.BlockSpec((1, tm, d), lambda b, i: (b, i, col))
    vec = pl.BlockSpec((1, d), lambda b, i: (0, 0))
    mat = pl.BlockSpec((d, d), lambda b, i: (0, 0))
    gate = pl.BlockSpec((1, tmm, d), gate_idx)
    return (bx, t // tm), row, vec, mat, gate


def _s5_post(y, uz, d_skip, w_glu, b_glu, w_out, g_post, m, x):
    grid, row, vec, mat, gate = _row_specs(x, m)
    d = x.shape[-1]
    return pl.pallas_call(
        _s5_post_kernel,
        grid=grid,
        in_specs=[row(0), row(0), row(1), vec, mat, vec, mat, vec, gate, row(0)],
        out_specs=row(0),
        out_shape=jax.ShapeDtypeStruct(x.shape, F32),
        compiler_params=_params("parallel", "parallel"),
        name="s5_post",
    )(y, uz, uz, d_skip.reshape(1, d), w_glu, b_glu.reshape(1, d), w_out, g_post.reshape(1, d), m, x)


def _sb_post(a, z, w_out, g_post, m, x):
    grid, row, vec, mat, gate = _row_specs(x, m)
    d = x.shape[-1]
    return pl.pallas_call(
        _sb_post_kernel,
        grid=grid,
        in_specs=[row(0), row(0), mat, vec, gate, row(0)],
        out_specs=row(0),
        out_shape=jax.ShapeDtypeStruct(x.shape, F32),
        compiler_params=_params("parallel", "parallel"),
        name="sb_post",
    )(a, z, w_out, g_post.reshape(1, d), m, x)


def _s5_prep_kernel(lre_ref, lim_ref, ldt_ref, btr_ref, bti_ref, cre_ref, cim_ref,
                    a_ref, al_ref, bbr_ref, bbi_ref, wsr_ref, wsi_ref, catr_ref, catn_ref, kft_ref):
    L = SSM_CHUNK
    lr, li = lre_ref[0], lim_ref[0]
    dt = jnp.exp(ldt_ref[0])
    mag = jnp.exp(lr * dt)
    ar, ai = mag * jnp.cos(li * dt), mag * jnp.sin(li * dt)
    den = lr * lr + li * li
    cr = ((ar - 1.0) * lr + ai * li) / den
    ci = (ai * lr - (ar - 1.0) * li) / den
    btr, bti = btr_ref[0], bti_ref[0]
    bbr, bbi = cr * btr - ci * bti, cr * bti + ci * btr
    cre, cim = cre_ref[0], cim_ref[0]
    a_ref[0, 0:1, :] = ar
    a_ref[0, 1:2, :] = ai
    bbr_ref[0] = bbr
    bbi_ref[0] = bbi
    pr, pi = jnp.ones_like(ar), jnp.zeros_like(ar)
    for t in range(L + 1):
        rows = slice(t * GROUP_CH, (t + 1) * GROUP_CH)
        catr_ref[0, rows, :] = cre * pr - cim * pi
        catn_ref[0, rows, :] = -(cre * pi + cim * pr)
        if t < L:
            wrows = slice((L - 1 - t) * GROUP_CH, (L - t) * GROUP_CH)
            wsr_ref[0, wrows, :] = pr * bbr - pi * bbi
            wsi_ref[0, wrows, :] = pr * bbi + pi * bbr
            pr, pi = pr * ar - pi * ai, pr * ai + pi * ar
    al_ref[0, 0:1, :] = pr
    al_ref[0, 1:2, :] = pi
    kft_ref[0] = _dot3(catr_ref[0], bbr, _NT) + _dot3(catn_ref[0], bbi, _NT)


def _s5_prep(lam_re, lam_im, log_dt, bmat_re, bmat_im, cmat_re, cmat_im):
    g, p, gc, L = N_GROUPS, P_STATE, GROUP_CH, SSM_CHUNK
    vec = pl.BlockSpec((1, 1, p), lambda i: (i, 0, 0))
    mat = pl.BlockSpec((1, gc, p), lambda i: (i, 0, 0))
    blk = lambda r, c: pl.BlockSpec((1, r, c), lambda i: (i, 0, 0))
    shp = lambda r, c: jax.ShapeDtypeStruct((g, r, c), F32)
    return pl.pallas_call(
        _s5_prep_kernel,
        grid=(g,),
        in_specs=[vec, vec, pl.BlockSpec((1, 1, 1), lambda i: (i, 0, 0)), mat, mat, mat, mat],
        out_specs=[blk(2, p), blk(2, p), blk(gc, p), blk(gc, p), blk(L * gc, p), blk(L * gc, p),
                   blk((L + 1) * gc, p), blk((L + 1) * gc, p), blk((L + 1) * gc, gc)],
        out_shape=[shp(2, p), shp(2, p), shp(gc, p), shp(gc, p), shp(L * gc, p), shp(L * gc, p),
                   shp((L + 1) * gc, p), shp((L + 1) * gc, p), shp((L + 1) * gc, gc)],
        compiler_params=_params("parallel"),
        name="s5_prep",
    )(lam_re.reshape(g, 1, p), lam_im.reshape(g, 1, p), log_dt.reshape(g, 1, 1),
      jnp.swapaxes(bmat_re, 1, 2), jnp.swapaxes(bmat_im, 1, 2), cmat_re, cmat_im)


def _s5_operators(prep):
    a, al, bbr, bbi, wsr, wsi, catr, catn, kft = prep
    g, p, gc, L = N_GROUPS, P_STATE, GROUP_CH, SSM_CHUNK
    k4 = kft[:, :L * gc].reshape(g, L, gc, gc)
    s_idx = jnp.arange(L)
    lag = s_idx[None, :] - s_idx[:, None]
    m5 = jnp.where((lag >= 0)[None, :, :, None, None], k4[:, jnp.clip(lag, 0, L - 1)], 0.0)
    m = m5.transpose(0, 1, 4, 2, 3).reshape(g, L * gc, L * gc)
    wy_re = jnp.swapaxes(catr[:, gc:], 1, 2)
    wy_im = jnp.swapaxes(catn[:, gc:], 1, 2)
    prompt_ops = (wsr.astype(BF16), wsi.astype(BF16), m.astype(BF16),
                  wy_re.astype(BF16), wy_im.astype(BF16), al[:, 0:1], al[:, 1:2])
    no, go = g // GROUPS_PER_OCTET, GROUPS_PER_OCTET
    eye = jnp.eye(go, dtype=F32)

    def in_bd(bt):
        return jnp.einsum('ogcp,gh->ogchp', bt.reshape(no, go, gc, p), eye).reshape(no, go * gc, go * p)

    def out_bd(ct):
        return jnp.einsum('ogcp,gh->ogphc', ct.reshape(no, go, gc, p), eye).reshape(no, go * p, go * gc)

    sample_ops = (a[:, 0].reshape(1, g * p), a[:, 1].reshape(1, g * p), in_bd(bbr), in_bd(bbi),
                  out_bd(catr[:, :gc]), out_bd(catn[:, :gc]))
    return prompt_ops, sample_ops


def _s5_scan_kernel(ut_ref, wsr_ref, wsi_ref, m_ref, wyr_ref, wyi_ref, alr_ref, ali_ref,
                    y_ref, hr_ref, hi_ref, s_re, s_im, hp_re, hp_im, *, n_chunks, n_b):
    ut = ut_ref[0]
    s_re[...] = _dot(ut, wsr_ref[0])
    s_im[...] = _dot(ut, wsi_ref[0])
    alr = jnp.broadcast_to(alr_ref[0], (n_b, P_STATE))
    ali = jnp.broadcast_to(ali_ref[0], (n_b, P_STATE))

    def step(k, carry):
        hr, hi = carry
        rows = pl.ds(pl.multiple_of(k * n_b, n_b), n_b)
        hp_re[rows, :] = hr
        hp_im[rows, :] = hi
        return (alr * hr - ali * hi + s_re[rows, :], alr * hi + ali * hr + s_im[rows, :])

    zero = jnp.zeros((n_b, P_STATE), F32)
    hr, hi = lax.fori_loop(0, n_chunks, step, (zero, zero))
    hr_ref[0] = hr
    hi_ref[0] = hi
    y_ref[0] = (_dot(ut, m_ref[0]) + _dot(hp_re[...].astype(BF16), wyr_ref[0])
                + _dot(hp_im[...].astype(BF16), wyi_ref[0]))


def _s5_scan(u, ops):
    wsr, wsi, m, wyr, wyi, alr, ali = ops
    n_b, t, e = u.shape
    g, p, gc, L = N_GROUPS, P_STATE, GROUP_CH, SSM_CHUNK
    nc = t // L
    rows, cols = nc * n_b, L * gc
    assert n_b % SUBLANES == 0 and t % L == 0
    ut = u.astype(BF16).reshape(n_b, nc, L, g, gc).transpose(3, 1, 0, 2, 4).reshape(g, rows, cols)
    blk = lambda r, c: pl.BlockSpec((1, r, c), lambda i: (i, 0, 0))
    yt, hr, hi = pl.pallas_call(
        functools.partial(_s5_scan_kernel, n_chunks=nc, n_b=n_b),
        grid=(g,),
        in_specs=[blk(rows, cols), blk(cols, p), blk(cols, p), blk(cols, cols), blk(p, cols),
                  blk(p, cols), blk(1, p), blk(1, p)],
        out_specs=[blk(rows, cols), blk(n_b, p), blk(n_b, p)],
        out_shape=[jax.ShapeDtypeStruct((g, rows, cols), F32),
                   jax.ShapeDtypeStruct((g, n_b, p), F32),
                   jax.ShapeDtypeStruct((g, n_b, p), F32)],
        scratch_shapes=[pltpu.VMEM((rows, p), F32)] * 4,
        compiler_params=_params("parallel"),
        name="s5_scan",
    )(ut, wsr, wsi, m, wyr, wyi, alr, ali)
    y = yt.reshape(g, nc, n_b, L, gc).transpose(2, 1, 3, 0, 4).reshape(n_b, t, e)
    return y, hr.transpose(1, 0, 2), hi.transpose(1, 0, 2)


def _s5_step_kernel(u_ref, h0r_ref, h0i_ref, ar_ref, ai_ref, bbr_ref, bbi_ref, cr_ref, cn_ref,
                    y_ref, hr_ref, hi_ref):
    u = u_ref[...]
    ar, ai = ar_ref[...], ai_ref[...]
    h0r, h0i = h0r_ref[...], h0i_ref[...]
    hr = ar * h0r - ai * h0i + _dot3(u, bbr_ref[0])
    hi = ar * h0i + ai * h0r + _dot3(u, bbi_ref[0])
    hr_ref[...] = hr
    hi_ref[...] = hi
    y_ref[...] = _dot3(hr, cr_ref[0]) + _dot3(hi, cn_ref[0])


def _s5_step(u, h0_re, h0_im, ops):
    ar, ai, bbr, bbi, cr, cn = ops
    s, e = u.shape
    g, p = N_GROUPS, P_STATE
    no = g // GROUPS_PER_OCTET
    wc, ws = GROUPS_PER_OCTET * GROUP_CH, GROUPS_PER_OCTET * p
    assert wc == LANES
    ch = pl.BlockSpec((s, wc), lambda o: (0, o))
    st = pl.BlockSpec((s, ws), lambda o: (0, o))
    av = pl.BlockSpec((1, ws), lambda o: (0, o))
    w_in = pl.BlockSpec((1, wc, ws), lambda o: (o, 0, 0))
    w_out = pl.BlockSpec((1, ws, wc), lambda o: (o, 0, 0))
    y, hr, hi = pl.pallas_call(
        _s5_step_kernel,
        grid=(no,),
        in_specs=[ch, st, st, av, av, w_in, w_in, w_out, w_out],
        out_specs=[ch, st, st],
        out_shape=[jax.ShapeDtypeStruct((s, e), F32),
                   jax.ShapeDtypeStruct((s, g * p), F32),
                   jax.ShapeDtypeStruct((s, g * p), F32)],
        compiler_params=_params("parallel"),
        name="s5_step",
    )(u, h0_re.reshape(s, g * p), h0_im.reshape(s, g * p), ar, ai, bbr, bbi, cr, cn)
    return y, hr.reshape(s, g, p), hi.reshape(s, g, p)


def _sb_prompt_kernel(bias_ref, q_ref, k_ref, v_ref, uo_ref, o_ref, acc, carry, *, tile):
    hp, qi = pl.program_id(1), pl.program_id(2)
    q2 = q_ref[0]
    lane = lax.broadcasted_iota(jnp.int32, q2.shape, 1)
    zero = jnp.zeros_like(q2)
    qm = (jnp.where(lane < HEAD_DIM, q2, zero), jnp.where(lane >= HEAD_DIM, q2, zero))
    bias = (bias_ref[2 * hp], bias_ref[2 * hp + 1])
    uo = uo_ref[...]
    row = lax.broadcasted_iota(jnp.int32, (tile, tile), 0)
    col = lax.broadcasted_iota(jnp.int32, (tile, tile), 1)
    below = col < row

    def visit(j, diagonal):
        ks = pl.ds(pl.multiple_of(j * tile, tile), tile)
        kb, vb = k_ref[0, ks, :], v_ref[0, ks, :]
        for i in range(2):
            z = lax.dot_general(qm[i], kb, _NT, preferred_element_type=F32) + bias[i]
            sp = _softplus(z)
            spm = jnp.where(below, sp, 0.0) if diagonal else sp
            hi, lo = _split_bf16(spm)
            cr = _dot(hi, uo) + _dot(lo, uo)
            if diagonal:
                w = jnp.where(below, jnp.exp(z - sp + cr[:, :tile]), 0.0)
                acc[i] = _dot(w.astype(BF16), vb)
                carry[i] = cr[:, tile:]
            else:
                w = jnp.exp(z - sp + (carry[i] + cr[:, :tile]))
                acc[i] += _dot(w.astype(BF16), vb)
                carry[i] += cr[:, tile:]

    visit(qi, True)

    def body(t, c):
        visit(qi - 1 - t, False)
        return c

    lax.fori_loop(0, qi, body, 0)
    o_ref[0] = jnp.where(lane < HEAD_DIM, acc[0], acc[1])


def _strict_upper_and_ones(n):
    j = jnp.arange(n)[:, None]
    s = jnp.arange(n)[None, :]
    return jnp.concatenate([-(j > s).astype(F32), -jnp.ones((n, n), F32)], axis=1).astype(BF16)


def _sb_prompt(q, k, v, bias):
    n_b, t, e = q.shape
    tile = min(ATTN_TILE, t)
    assert 2 * HEAD_DIM == LANES and t % tile == 0
    kv = pl.BlockSpec((1, t, LANES), lambda b, h, i: (b, 0, h))
    qo = pl.BlockSpec((1, tile, LANES), lambda b, h, i: (b, i, h))
    return pl.pallas_call(
        functools.partial(_sb_prompt_kernel, tile=tile),
        grid=(n_b, e // LANES, t // tile),
        in_specs=[pl.BlockSpec(memory_space=pltpu.SMEM), qo, kv, kv,
                  pl.BlockSpec((tile, 2 * tile), lambda b, h, i: (0, 0))],
        out_specs=qo,
        out_shape=jax.ShapeDtypeStruct((n_b, t, e), F32),
        scratch_shapes=[pltpu.VMEM((2, tile, LANES), F32), pltpu.VMEM((2, tile, tile), F32)],
        compiler_params=_params("parallel", "parallel", "parallel"),
        name="sb_prompt",
    )(bias, q, k, v, _strict_upper_and_ones(tile))


def _sb_sample_kernel(pt_ref, qbd_ref, bias_ref, lo_ref, ex_ref, *refs, n_pp, page):
    k_refs, v_refs = refs[:n_pp], refs[n_pp:2 * n_pp]
    o_ref, carry, acc = refs[2 * n_pp:]
    step = pl.program_id(1)

    @pl.when(step == 0)
    def _():
        carry[...] = jnp.zeros_like(carry)
        acc[...] = jnp.zeros_like(acc)

    qbd = qbd_ref[0]
    lo_mat = lo_ref[...]
    for i in range(n_pp):
        kp = k_refs[i][0].astype(BF16)
        z = lax.dot_general(kp, qbd, _NT, preferred_element_type=F32) + bias_ref[...]
        sp = _softplus(z)
        hi, lo = _split_bf16(sp)
        cr = _dot(lo_mat, hi) + _dot(lo_mat, lo)
        w = jnp.exp(z - sp + (carry[...] + cr[:page]))
        carry[...] += cr[page:]
        wv = _dot(w.astype(BF16), ex_ref[...]) * v_refs[i][0]
        acc[...] += wv.reshape(page // SUBLANES, SUBLANES, wv.shape[-1]).sum(axis=0)

    @pl.when(step == pl.num_programs(1) - 1)
    def _():
        o_ref[0] = jnp.sum(acc[...], axis=0, keepdims=True)


def _sb_sample(q, cache_k, cache_v, page_table, bias):
    s, e = q.shape
    n_phys, page = cache_k.shape[:2]
    n_pages = page_table.shape[1]
    n_pp = PAGES_PER_STEP if n_pages % PAGES_PER_STEP == 0 else 1
    h = N_HEADS
    head_of = jnp.arange(e) // HEAD_DIM
    sel = (head_of[None, :] == jnp.arange(h)[:, None])
    qbd = jnp.where(sel[None], q[:, None, :], 0.0).astype(BF16)
    expand = sel.astype(BF16)
    t1 = jnp.arange(page)
    later = (t1[None, :] > t1[:, None]).astype(F32)
    lo_mat = jnp.concatenate([-later, -jnp.ones((page, page), F32)], axis=0).astype(BF16)

    def page_spec(i):
        def idx(b, g, pt):
            return (pt[b * n_pages + (n_pages - 1 - (g * n_pp + i))], 0, 0)
        return pl.BlockSpec((1, page, e), idx)

    pages = [page_spec(i) for i in range(n_pp)]
    const = lambda shape: pl.BlockSpec(shape, lambda b, g, pt: (0,) * len(shape))
    out = pl.pallas_call(
        functools.partial(_sb_sample_kernel, n_pp=n_pp, page=page),
        grid_spec=pltpu.PrefetchScalarGridSpec(
            num_scalar_prefetch=1,
            grid=(s, n_pages // n_pp),
            in_specs=[pl.BlockSpec((1, h, e), lambda b, g, pt: (b, 0, 0)),
                      const((1, h)), const((2 * page, page)), const((h, e))] + pages + pages,
            out_specs=pl.BlockSpec((1, 1, e), lambda b, g, pt: (b, 0, 0)),
            scratch_shapes=[pltpu.VMEM((page, h), F32), pltpu.VMEM((SUBLANES, e), F32)]),
        out_shape=jax.ShapeDtypeStruct((s, 1, e), F32),
        compiler_params=_params("parallel", "arbitrary"),
        name="sb_sample",
    )(page_table.reshape(-1), qbd, bias.reshape(1, h), lo_mat, expand,
      *([cache_k.reshape(n_phys, page, e)] * n_pp), *([cache_v.reshape(n_phys, page, e)] * n_pp))
    return out.reshape(s, e)


def kernel(x_prompt, x_sample, c_prompt, c_sample, state_ssm_re, state_ssm_im, cache_k, cache_v, page_table, g_pre_a, g_post_a, w_mod_a, b_mod_a, w_in_a, lam_re, lam_im, log_dt, bmat_re, bmat_im, cmat_re, cmat_im, d_skip, w_glu, b_glu, w_out_a, g_kv, w_mod_kv, b_mod_kv, w_kv, g_pre_b, g_post_b, w_mod_b, b_mod_b, w_in_b, sb_bias, w_out_b):
    n_b, t, d = x_prompt.shape
    n_s = x_sample.shape[0]
    n_a, n_bl = w_in_a.shape[0], w_in_b.shape[0]
    e = d
    q_scale = HEAD_DIM ** -0.5
    bf = lambda w: w.astype(BF16)

    c_all = jnp.concatenate([c_prompt, c_sample], axis=0)

    def mods(w_mod, b_mod):
        m = _ada_mod(c_all, bf(w_mod), b_mod)
        return m[:n_b].reshape(n_b, 1, -1), m[n_b:].reshape(1, n_s, -1)

    xp = x_prompt
    xs = x_sample.reshape(1, n_s, d)
    p_re, p_im, s_re, s_im = [], [], [], []
    uz_cfg = ((0, 2 * e, ((0, 1.0),)),)
    for i in range(n_a):
        mp, ms = mods(w_mod_a[i], b_mod_a[i])
        w_in, w_g, w_o = bf(w_in_a[i]), bf(w_glu[i]), bf(w_out_a[i])
        prompt_ops, sample_ops = _s5_operators(
            _s5_prep(lam_re[i], lam_im[i], log_dt[i], bmat_re[i], bmat_im[i], cmat_re[i], cmat_im[i]))
        (uz,) = _norm_mm(xp, mp, g_pre_a[i], w_in, uz_cfg, (F32,))
        y, hr, hi = _s5_scan(uz[..., :e], prompt_ops)
        xp = _s5_post(y, uz, d_skip[i], w_g, b_glu[i], w_o, g_post_a[i], mp, xp)
        p_re.append(hr)
        p_im.append(hi)
        (uz,) = _norm_mm(xs, ms, g_pre_a[i], w_in, uz_cfg, (F32,))
        y, hr, hi = _s5_step(uz[0, :, :e], state_ssm_re[i], state_ssm_im[i], sample_ops)
        xs = _s5_post(y.reshape(1, n_s, e), uz, d_skip[i], w_g, b_glu[i], w_o, g_post_a[i], ms, xs)
        s_re.append(hr)
        s_im.append(hi)

    mp, ms = mods(w_mod_kv, b_mod_kv)
    w = bf(w_kv)
    kv_cfg = ((0, e, ((0, 1.0), (2, 1.0))), (e, e, ((1, 1.0), (3, 1.0))))
    p_k, p_v, pk_bf, pv_bf = _norm_mm(xp, mp, g_kv, w, kv_cfg, (F32, F32, BF16, BF16))
    kv_cfg_s = ((0, e, ((0, 1.0),)), (e, e, ((1, 1.0),)))
    s_k, s_v = _norm_mm(xs, ms, g_kv, w, kv_cfg_s, (F32, F32))

    qz_cfg = ((0, e, ((0, q_scale),)), (e, e, ((1, 1.0),)))
    for j in range(n_bl):
        mp, ms = mods(w_mod_b[j], b_mod_b[j])
        w_in, w_o = bf(w_in_b[j]), bf(w_out_b[j])
        q, z = _norm_mm(xp, mp, g_pre_b[j], w_in, qz_cfg, (BF16, F32))
        a = _sb_prompt(q, pk_bf, pv_bf, sb_bias[j])
        xp = _sb_post(a, z, w_o, g_post_b[j], mp, xp)
        q, z = _norm_mm(xs, ms, g_pre_b[j], w_in, qz_cfg, (F32, F32))
        a = _sb_sample(q[0], cache_k, cache_v, page_table, sb_bias[j])
        xs = _sb_post(a.reshape(1, n_s, e), z, w_o, g_post_b[j], ms, xs)

    return (xp, xs.reshape(n_s, 1, d), jnp.stack(p_re), jnp.stack(p_im),
            p_k.reshape(n_b, t, N_HEADS, HEAD_DIM), p_v.reshape(n_b, t, N_HEADS, HEAD_DIM),
            jnp.stack(s_re), jnp.stack(s_im),
            s_k.reshape(n_s, 1, N_HEADS, HEAD_DIM), s_v.reshape(n_s, 1, N_HEADS, HEAD_DIM))
```

```python
import functools

import jax
import jax.numpy as jnp
from jax import lax
from jax.experimental import pallas as pl
from jax.experimental.pallas import tpu as pltpu

F32 = jnp.float32
BF16 = jnp.bfloat16

D_MODEL = 1024
GROUP_CH = 16
N_GROUPS = D_MODEL // GROUP_CH
P_STATE = 64
N_HEADS = 16
HEAD_DIM = D_MODEL // N_HEADS
EPS = 1e-6

SSM_CHUNK = 8
S5_TIME_TILE = 512
GROUPS_PER_OCTET = 8
LANES = 128
SUBLANES = 8
ROW_TILE = 512
ATTN_TILE = 256
ATTN_HEAD_PAIRS = 2
LOG2E = 1.4426950408889634
PAGES_PER_STEP = 8
VMEM_LIMIT = 48 * 1024 * 1024

_NT = (((1,), (1,)), ((), ()))


def _params(*sem):
    return pltpu.CompilerParams(dimension_semantics=sem, vmem_limit_bytes=VMEM_LIMIT)


def _sigmoid(x):
    return 1.0 / (1.0 + jnp.exp(-x))


def _softplus(z):
    return jnp.maximum(z, 0.0) + jnp.log(1.0 + jnp.exp(-jnp.abs(z)))


def _softplus2(z):
    return jnp.maximum(z, 0.0) + jnp.log2(1.0 + jnp.exp2(-jnp.abs(z)))


def _gelu_tanh(x):
    return 0.5 * x * (1.0 + jnp.tanh(0.7978845608028654 * (x + 0.044715 * (x * x * x))))


def _split_bf16(x):
    hi = x.astype(BF16)
    lo = (x - hi.astype(F32)).astype(BF16)
    return hi, lo


def _dot(a, b):
    return jnp.dot(a, b, preferred_element_type=F32)


def _dot3(a, b, dims=None):
    a_hi, a_lo = _split_bf16(a)
    b_hi, b_lo = _split_bf16(b)
    if dims is None:
        f = _dot
    else:
        f = lambda p, q: lax.dot_general(p, q, dims, preferred_element_type=F32)
    return f(a_hi, b_hi) + (f(a_hi, b_lo) + f(a_lo, b_hi))


def _rms(x, g):
    return x * lax.rsqrt(jnp.mean(x * x, axis=-1, keepdims=True) + EPS) * g


def _mod_kernel(c_ref, w_ref, b_ref, o_ref):
    c = c_ref[...]
    s = (c * _sigmoid(c)).astype(BF16)
    o_ref[...] = _dot(s, w_ref[...]) + b_ref[...]


def _ada_mod(c, w, b):
    m, d = c.shape
    n = w.shape[1]
    tn = 1024
    return pl.pallas_call(
        _mod_kernel,
        grid=(n // tn,),
        in_specs=[pl.BlockSpec((m, d), lambda j: (0, 0)),
                  pl.BlockSpec((d, tn), lambda j: (0, j)),
                  pl.BlockSpec((1, tn), lambda j: (0, j))],
        out_specs=pl.BlockSpec((m, tn), lambda j: (0, j)),
        out_shape=jax.ShapeDtypeStruct((m, n), F32),
        compiler_params=_params("parallel"),
        name="ada_mod",
    )(c, w, b.reshape(1, n))


def _norm_mm_kernel(x_ref, sh_ref, sc_ref, g_ref, w_ref, *out_refs, cfg, nchunk):
    h = _rms(x_ref[0], g_ref[...]) * (1.0 + sc_ref[0]) + sh_ref[0]
    hb = h.astype(BF16)
    for col0, ncols, outs in cfg:
        for j in range(0, ncols, nchunk):
            r = _dot(hb, w_ref[:, col0 + j:col0 + j + nchunk])
            for oi, scale in outs:
                v = r if scale == 1.0 else r * scale
                out_refs[oi][0, :, j:j + nchunk] = v.astype(out_refs[oi].dtype)


def _norm_mm(x, m, g, w, cfg, out_dtypes):
    bx, t, d = x.shape
    tm = min(ROW_TILE, t)
    per_row = m.shape[1] != 1
    tmm = tm if per_row else 1
    mod_idx = (lambda col: (lambda b, i: (b, i, col))) if per_row else (lambda col: (lambda b, i: (b, 0, col)))
    widths = {}
    for _, ncols, outs in cfg:
        for oi, _ in outs:
            widths[oi] = ncols
    n_out = len(out_dtypes)
    return pl.pallas_call(
        functools.partial(_norm_mm_kernel, cfg=cfg, nchunk=512),
        grid=(bx, t // tm),
        in_specs=[pl.BlockSpec((1, tm, d), lambda b, i: (b, i, 0)),
                  pl.BlockSpec((1, tmm, d), mod_idx(0)),
                  pl.BlockSpec((1, tmm, d), mod_idx(1)),
                  pl.BlockSpec((1, d), lambda b, i: (0, 0)),
                  pl.BlockSpec(w.shape, lambda b, i: (0, 0))],
        out_specs=[pl.BlockSpec((1, tm, widths[oi]), lambda b, i: (b, i, 0)) for oi in range(n_out)],
        out_shape=[jax.ShapeDtypeStruct((bx, t, widths[oi]), out_dtypes[oi]) for oi in range(n_out)],
        compiler_params=_params("parallel", "parallel"),
        name="norm_proj",
    )(x, m, m, g.reshape(1, d), w)


def _residual_tail(y, w_out_ref, g_ref, gate_ref, x_ref, o_ref):
    o = _dot(y.astype(BF16), w_out_ref[...])
    o_ref[0] = x_ref[0] + gate_ref[0] * _rms(o, g_ref[...])


def _s5_post_kernel(y_ref, u_ref, z_ref, dsk_ref, wglu_ref, bglu_ref, wout_ref, g_ref, gate_ref,
                    x_ref, o_ref):
    y = _gelu_tanh(y_ref[0] + dsk_ref[...] * u_ref[0])
    y = y * _sigmoid(_dot(y.astype(BF16), wglu_ref[...]) + bglu_ref[...])
    z = z_ref[0]
    y = y * (z * _sigmoid(z))
    _residual_tail(y, wout_ref, g_ref, gate_ref, x_ref, o_ref)


def _sb_post_kernel(a_ref, z_ref, wout_ref, g_ref, gate_ref, x_ref, o_ref):
    z = z_ref[0]
    y = a_ref[0] * (z * _sigmoid(z))
    _residual_tail(y, wout_ref, g_ref, gate_ref, x_ref, o_ref)


def _row_specs(x, m):
    bx, t, d = x.shape
    tm = min(ROW_TILE, t)
    per_row = m.shape[1] != 1
    tmm = tm if per_row else 1
    gate_idx = (lambda b, i: (b, i, 2)) if per_row else (lambda b, i: (b, 0, 2))
    row = lambda col: pl.BlockSpec((1, tm, d), lambda b, i: (b, i, col))
    vec = pl.BlockSpec((1, d), lambda b, i: (0, 0))
    mat = pl.BlockSpec((d, d), lambda b, i: (0, 0))
    gate = pl.BlockSpec((1, tmm, d), gate_idx)
    return (bx, t // tm), row, vec, mat, gate


def _s5_post(y, uz, d_skip, w_glu, b_glu, w_out, g_post, m, x):
    grid, row, vec, mat, gate = _row_specs(x, m)
    d = x.shape[-1]
    return pl.pallas_call(
        _s5_post_kernel,
        grid=grid,
        in_specs=[row(0), row(0), row(1), vec, mat, vec, mat, vec, gate, row(0)],
        out_specs=row(0),
        out_shape=jax.ShapeDtypeStruct(x.shape, F32),
        compiler_params=_params("parallel", "parallel"),
        name="s5_post",
    )(y, uz, uz, d_skip.reshape(1, d), w_glu, b_glu.reshape(1, d), w_out, g_post.reshape(1, d), m, x)


def _sb_post(a, z, w_out, g_post, m, x):
    grid, row, vec, mat, gate = _row_specs(x, m)
    d = x.shape[-1]
    return pl.pallas_call(
        _sb_post_kernel,
        grid=grid,
        in_specs=[row(0), row(0), mat, vec, gate, row(0)],
        out_specs=row(0),
        out_shape=jax.ShapeDtypeStruct(x.shape, F32),
        compiler_params=_params("parallel", "parallel"),
        name="sb_post",
    )(a, z, w_out, g_post.reshape(1, d), m, x)


def _s5_prep_kernel(lre_ref, lim_ref, ldt_ref, btr_ref, bti_ref, cre_ref, cim_ref,
                    a_ref, al_ref, bbr_ref, bbi_ref, wsr_ref, wsi_ref, catr_ref, catn_ref, kft_ref):
    L = SSM_CHUNK
    lr, li = lre_ref[0], lim_ref[0]
    dt = jnp.exp(ldt_ref[0])
    mag = jnp.exp(lr * dt)
    ar, ai = mag * jnp.cos(li * dt), mag * jnp.sin(li * dt)
    den = lr * lr + li * li
    cr = ((ar - 1.0) * lr + ai * li) / den
    ci = (ai * lr - (ar - 1.0) * li) / den
    btr, bti = btr_ref[0], bti_ref[0]
    bbr, bbi = cr * btr - ci * bti, cr * bti + ci * btr
    cre, cim = cre_ref[0], cim_ref[0]
    a_ref[0, 0:1, :] = ar
    a_ref[0, 1:2, :] = ai
    bbr_ref[0] = bbr
    bbi_ref[0] = bbi
    pr, pi = jnp.ones_like(ar), jnp.zeros_like(ar)
    for t in range(L + 1):
        rows = slice(t * GROUP_CH, (t + 1) * GROUP_CH)
        catr_ref[0, rows, :] = cre * pr - cim * pi
        catn_ref[0, rows, :] = -(cre * pi + cim * pr)
        if t < L:
            wrows = slice((L - 1 - t) * GROUP_CH, (L - t) * GROUP_CH)
            wsr_ref[0, wrows, :] = pr * bbr - pi * bbi
            wsi_ref[0, wrows, :] = pr * bbi + pi * bbr
            pr, pi = pr * ar - pi * ai, pr * ai + pi * ar
    al_ref[0, 0:1, :] = pr
    al_ref[0, 1:2, :] = pi
    kft_ref[0] = _dot3(catr_ref[0], bbr, _NT) + _dot3(catn_ref[0], bbi, _NT)


def _s5_prep(lam_re, lam_im, log_dt, bmat_re, bmat_im, cmat_re, cmat_im):
    g, p, gc, L = N_GROUPS, P_STATE, GROUP_CH, SSM_CHUNK
    vec = pl.BlockSpec((1, 1, p), lambda i: (i, 0, 0))
    mat = pl.BlockSpec((1, gc, p), lambda i: (i, 0, 0))
    blk = lambda r, c: pl.BlockSpec((1, r, c), lambda i: (i, 0, 0))
    shp = lambda r, c: jax.ShapeDtypeStruct((g, r, c), F32)
    return pl.pallas_call(
        _s5_prep_kernel,
        grid=(g,),
        in_specs=[vec, vec, pl.BlockSpec((1, 1, 1), lambda i: (i, 0, 0)), mat, mat, mat, mat],
        out_specs=[blk(2, p), blk(2, p), blk(gc, p), blk(gc, p), blk(L * gc, p), blk(L * gc, p),
                   blk((L + 1) * gc, p), blk((L + 1) * gc, p), blk((L + 1) * gc, gc)],
        out_shape=[shp(2, p), shp(2, p), shp(gc, p), shp(gc, p), shp(L * gc, p), shp(L * gc, p),
                   shp((L + 1) * gc, p), shp((L + 1) * gc, p), shp((L + 1) * gc, gc)],
        compiler_params=_params("parallel"),
        name="s5_prep",
    )(lam_re.reshape(g, 1, p), lam_im.reshape(g, 1, p), log_dt.reshape(g, 1, 1),
      jnp.swapaxes(bmat_re, 1, 2), jnp.swapaxes(bmat_im, 1, 2), cmat_re, cmat_im)


def _s5_operators(prep):
    a, al, bbr, bbi, wsr, wsi, catr, catn, kft = prep
    g, p, gc, L = N_GROUPS, P_STATE, GROUP_CH, SSM_CHUNK
    no, go = g // GROUPS_PER_OCTET, GROUPS_PER_OCTET
    eye = jnp.eye(go, dtype=F32)
    bd = lambda x, spec, rows: jnp.einsum(spec, x.reshape(no, go, L, gc, -1), eye).reshape(no, rows, -1)
    k_lag = bd(kft[:, :L * gc], 'ogtcd,gh->otgdhc', L * go * gc)
    w_state = jnp.concatenate([bd(wsr, 'oglcp,gh->olgchp', L * go * gc),
                               bd(wsi, 'oglcp,gh->olgchp', L * go * gc)], axis=-1)
    w_carry = jnp.concatenate([bd(catr[:, gc:], 'oglcp,gh->ogplhc', go * p),
                               bd(catn[:, gc:], 'oglcp,gh->ogplhc', go * p)], axis=1)
    prompt_ops = (k_lag.astype(BF16), w_state.astype(BF16), w_carry.astype(BF16),
                  al[:, 0].reshape(1, g * p), al[:, 1].reshape(1, g * p))

    def in_bd(bt):
        return jnp.einsum('ogcp,gh->ogchp', bt.reshape(no, go, gc, p), eye).reshape(no, go * gc, go * p)

    def out_bd(ct):
        return jnp.einsum('ogcp,gh->ogphc', ct.reshape(no, go, gc, p), eye).reshape(no, go * p, go * gc)

    sample_ops = (a[:, 0].reshape(1, g * p), a[:, 1].reshape(1, g * p), in_bd(bbr), in_bd(bbi),
                  out_bd(catr[:, :gc]), out_bd(catn[:, :gc]))
    return prompt_ops, sample_ops


def _s5_scan_kernel(u_ref, klag_ref, wst_ref, wcar_ref, alr_ref, ali_ref, y_ref, hr_ref, hi_ref,
                    lagged, chunk_in, s_re, s_im, hp_re, hp_im, from_state, st_re, st_im, *, n_b, tt):
    L, W = SSM_CHUNK, LANES
    nc = tt // L
    hw = st_re.shape[-1]

    @pl.when(pl.program_id(1) == 0)
    def _():
        st_re[...] = jnp.zeros_like(st_re)
        st_im[...] = jnp.zeros_like(st_im)

    sub = lax.broadcasted_iota(jnp.int32, (nc, L, W), 1)
    for b in range(n_b):
        x3 = u_ref[b].reshape(nc, L, W)
        for lag in range(L):
            shifted = x3 if lag == 0 else jnp.where(sub >= lag, pltpu.roll(x3, lag, axis=1), 0.0)
            lagged[:, lag * W:(lag + 1) * W] = shifted.reshape(tt, W).astype(BF16)
        y_ref[b] = _dot(lagged[...], klag_ref[0])
        for l in range(L):
            chunk_in[l, pl.ds(b, nc, stride=n_b), :] = u_ref[b, pl.ds(l, nc, stride=L), :]

    s = _dot(jnp.concatenate([chunk_in[l].astype(BF16) for l in range(L)], axis=1), wst_ref[0])
    s_re[...] = s[:, :hw]
    s_im[...] = s[:, hw:]
    alr = jnp.broadcast_to(alr_ref[...], (n_b, hw))
    ali = jnp.broadcast_to(ali_ref[...], (n_b, hw))

    def step(k, carry):
        hr, hi = carry
        rows = pl.ds(pl.multiple_of(k * n_b, n_b), n_b)
        hp_re[rows, :] = hr
        hp_im[rows, :] = hi
        return (alr * hr - ali * hi + s_re[rows, :], alr * hi + ali * hr + s_im[rows, :])

    hr, hi = lax.fori_loop(0, nc, step, (st_re[...], st_im[...]))
    st_re[...] = hr
    st_im[...] = hi
    hr_ref[...] = hr
    hi_ref[...] = hi
    r = (_dot(hp_re[...].astype(BF16), wcar_ref[0, :hw, :])
         + _dot(hp_im[...].astype(BF16), wcar_ref[0, hw:, :]))
    for l in range(L):
        from_state[l] = r[:, l * W:(l + 1) * W]
    for b in range(n_b):
        for l in range(L):
            y_ref[b, pl.ds(l, nc, stride=L), :] += from_state[l, pl.ds(b, nc, stride=n_b), :]


def _s5_scan(uz, ops):
    k_lag, w_state, w_carry, alr, ali = ops
    n_b, t, e2 = uz.shape
    e = e2 // 2
    g, p, L, W = N_GROUPS, P_STATE, SSM_CHUNK, LANES
    no = e // W
    hw = GROUPS_PER_OCTET * p
    tt = min(S5_TIME_TILE, t)
    assert n_b == SUBLANES and L == SUBLANES and t % tt == 0 and GROUPS_PER_OCTET * GROUP_CH == W
    rows = n_b * tt // L
    per_octet = lambda r, c: pl.BlockSpec((1, r, c), lambda o, i: (o, 0, 0))
    state_vec = pl.BlockSpec((1, hw), lambda o, i: (0, o))
    state_out = pl.BlockSpec((n_b, hw), lambda o, i: (0, o))
    io = pl.BlockSpec((n_b, tt, W), lambda o, i: (0, i, o))
    y, hr, hi = pl.pallas_call(
        functools.partial(_s5_scan_kernel, n_b=n_b, tt=tt),
        grid=(no, t // tt),
        in_specs=[io, per_octet(L * W, W), per_octet(L * W, 2 * hw), per_octet(2 * hw, L * W),
                  state_vec, state_vec],
        out_specs=[io, state_out, state_out],
        out_shape=[jax.ShapeDtypeStruct((n_b, t, e), F32),
                   jax.ShapeDtypeStruct((n_b, g * p), F32),
                   jax.ShapeDtypeStruct((n_b, g * p), F32)],
        scratch_shapes=[pltpu.VMEM((tt, L * W), BF16), pltpu.VMEM((L, rows, W), F32),
                        pltpu.VMEM((rows, hw), F32), pltpu.VMEM((rows, hw), F32),
                        pltpu.VMEM((rows, hw), F32), pltpu.VMEM((rows, hw), F32),
                        pltpu.VMEM((L, rows, W), F32),
                        pltpu.VMEM((n_b, hw), F32), pltpu.VMEM((n_b, hw), F32)],
        compiler_params=_params("parallel", "arbitrary"),
        name="s5_scan",
    )(uz, k_lag, w_state, w_carry, alr, ali)
    return y, hr.reshape(n_b, g, p), hi.reshape(n_b, g, p)


def _s5_step_kernel(u_ref, h0r_ref, h0i_ref, ar_ref, ai_ref, bbr_ref, bbi_ref, cr_ref, cn_ref,
                    y_ref, hr_ref, hi_ref):
    u = u_ref[...]
    ar, ai = ar_ref[...], ai_ref[...]
    h0r, h0i = h0r_ref[...], h0i_ref[...]
    hr = ar * h0r - ai * h0i + _dot3(u, bbr_ref[0])
    hi = ar * h0i + ai * h0r + _dot3(u, bbi_ref[0])
    hr_ref[...] = hr
    hi_ref[...] = hi
    y_ref[...] = _dot3(hr, cr_ref[0]) + _dot3(hi, cn_ref[0])


def _s5_step(u, h0_re, h0_im, ops):
    ar, ai, bbr, bbi, cr, cn = ops
    s, e = u.shape
    g, p = N_GROUPS, P_STATE
    no = g // GROUPS_PER_OCTET
    wc, ws = GROUPS_PER_OCTET * GROUP_CH, GROUPS_PER_OCTET * p
    assert wc == LANES
    ch = pl.BlockSpec((s, wc), lambda o: (0, o))
    st = pl.BlockSpec((s, ws), lambda o: (0, o))
    av = pl.BlockSpec((1, ws), lambda o: (0, o))
    w_in = pl.BlockSpec((1, wc, ws), lambda o: (o, 0, 0))
    w_out = pl.BlockSpec((1, ws, wc), lambda o: (o, 0, 0))
    y, hr, hi = pl.pallas_call(
        _s5_step_kernel,
        grid=(no,),
        in_specs=[ch, st, st, av, av, w_in, w_in, w_out, w_out],
        out_specs=[ch, st, st],
        out_shape=[jax.ShapeDtypeStruct((s, e), F32),
                   jax.ShapeDtypeStruct((s, g * p), F32),
                   jax.ShapeDtypeStruct((s, g * p), F32)],
        compiler_params=_params("parallel"),
        name="s5_step",
    )(u, h0_re.reshape(s, g * p), h0_im.reshape(s, g * p), ar, ai, bbr, bbi, cr, cn)
    return y, hr.reshape(s, g, p), hi.reshape(s, g, p)


def _sb_prompt_kernel(bias_ref, q_ref, k_ref, v_ref, u_ref, o_ref, acc, carry, *, tile, pairs):
    hg, qi = pl.program_id(1), pl.program_id(2)
    heads = range(2 * pairs)
    lanes = [slice((h // 2) * LANES, (h // 2 + 1) * LANES) for h in heads]
    first = lax.broadcasted_iota(jnp.int32, (tile, LANES), 1) < HEAD_DIM
    qm, bias = [], []
    for p in range(pairs):
        q2 = q_ref[0, :, p * LANES:(p + 1) * LANES]
        zero = jnp.zeros_like(q2)
        qm += [jnp.where(first, q2, zero), jnp.where(first, zero, q2)]
        bias += [bias_ref[2 * (hg * pairs + p)] * LOG2E, bias_ref[2 * (hg * pairs + p) + 1] * LOG2E]
    u = u_ref[...]
    row = lax.broadcasted_iota(jnp.int32, (tile, tile), 0)
    col = lax.broadcasted_iota(jnp.int32, (tile, tile), 1)
    below = col < row

    def visit(j, diagonal):
        ks = pl.ds(pl.multiple_of(j * tile, tile), tile)
        z = [lax.dot_general(qm[h], k_ref[0, ks, lanes[h]], _NT, preferred_element_type=F32) + bias[h]
             for h in heads]
        sp = [_softplus2(z[h]) for h in heads]
        log_beta = [z[h] - sp[h] for h in heads]
        if diagonal:
            sp = [jnp.where(below, sp[h], 0.0) for h in heads]
        parts = [_split_bf16(sp[h]) for h in heads]
        cum = [_dot(parts[h][0], u) + _dot(parts[h][1], u) for h in heads]
        total = [jnp.broadcast_to(cum[h][:, 0:1] - sp[h][:, 0:1], (tile, LANES)) for h in heads]
        if diagonal:
            w = [jnp.where(below, jnp.exp2(log_beta[h] + cum[h]), 0.0) for h in heads]
        else:
            w = [jnp.exp2(log_beta[h] + (cum[h] + jnp.concatenate([carry[h]] * (tile // LANES), axis=1)))
                 for h in heads]
        for h in heads:
            pv = _dot(w[h].astype(BF16), v_ref[0, ks, lanes[h]])
            if diagonal:
                acc[h] = pv
                carry[h] = total[h]
            else:
                acc[h] += pv
                carry[h] += total[h]

    visit(qi, True)

    def body(t, c):
        visit(qi - 1 - t, False)
        return c

    lax.fori_loop(0, qi, body, 0)
    for p in range(pairs):
        o_ref[0, :, p * LANES:(p + 1) * LANES] = jnp.where(first, acc[2 * p], acc[2 * p + 1])


def _sb_prompt(q, k, v, bias):
    n_b, t, e = q.shape
    tile = min(ATTN_TILE, t)
    pairs = ATTN_HEAD_PAIRS
    width = pairs * LANES
    assert 2 * HEAD_DIM == LANES and t % tile == 0 and tile % LANES == 0 and e % width == 0
    later = jnp.arange(tile)[:, None] > jnp.arange(tile)[None, :]
    u = -later.astype(BF16)
    kv = pl.BlockSpec((1, t, width), lambda b, h, i: (b, 0, h))
    qo = pl.BlockSpec((1, tile, width), lambda b, h, i: (b, i, h))
    return pl.pallas_call(
        functools.partial(_sb_prompt_kernel, tile=tile, pairs=pairs),
        grid=(n_b, e // width, t // tile),
        in_specs=[pl.BlockSpec(memory_space=pltpu.SMEM), qo, kv, kv,
                  pl.BlockSpec((tile, tile), lambda b, h, i: (0, 0))],
        out_specs=qo,
        out_shape=jax.ShapeDtypeStruct((n_b, t, e), F32),
        scratch_shapes=[pltpu.VMEM((2 * pairs, tile, LANES), F32),
                        pltpu.VMEM((2 * pairs, tile, LANES), F32)],
        compiler_params=_params("parallel", "parallel", "parallel"),
        name="sb_prompt",
    )(bias, q, k, v, u)


def _sb_sample_kernel(pt_ref, qbd_ref, bias_ref, lo_ref, ex_ref, *refs, n_pp, page):
    k_refs, v_refs = refs[:n_pp], refs[n_pp:2 * n_pp]
    o_ref, carry, acc = refs[2 * n_pp:]
    step = pl.program_id(1)

    @pl.when(step == 0)
    def _():
        carry[...] = jnp.zeros_like(carry)
        acc[...] = jnp.zeros_like(acc)

    qbd = qbd_ref[0]
    later = lo_ref[...]
    pages = range(n_pp)
    z = [lax.dot_general(k_refs[i][0].astype(BF16), qbd, _NT, preferred_element_type=F32) + bias_ref[...]
         for i in pages]
    sp = [_softplus(z[i]) for i in pages]
    parts = [_split_bf16(sp[i]) for i in pages]
    cum = [_dot(later, parts[i][0]) + _dot(later, parts[i][1]) for i in pages]
    run = carry[...]
    w = []
    for i in pages:
        w.append(jnp.exp(z[i] - sp[i] + (cum[i] + run)))
        run = run + (cum[i][0:1] - sp[i][0:1])
    carry[...] = run
    part = acc[...]
    for i in pages:
        wv = _dot(w[i].astype(BF16), ex_ref[...]) * v_refs[i][0]
        part = part + wv.reshape(page // SUBLANES, SUBLANES, wv.shape[-1]).sum(axis=0)
    acc[...] = part

    @pl.when(step == pl.num_programs(1) - 1)
    def _():
        o_ref[0] = jnp.sum(acc[...], axis=0, keepdims=True)


def _sb_sample(q, cache_k, cache_v, page_table, bias):
    s, e = q.shape
    n_phys, page = cache_k.shape[:2]
    n_pages = page_table.shape[1]
    n_pp = PAGES_PER_STEP if n_pages % PAGES_PER_STEP == 0 else 1
    h = N_HEADS
    head_of = jnp.arange(e) // HEAD_DIM
    sel = (head_of[None, :] == jnp.arange(h)[:, None])
    qbd = jnp.where(sel[None], q[:, None, :], 0.0).astype(BF16)
    expand = sel.astype(BF16)
    t1 = jnp.arange(page)
    later = -(t1[None, :] > t1[:, None]).astype(BF16)

    def page_spec(i):
        def idx(b, g, pt):
            return (pt[b * n_pages + (n_pages - 1 - (g * n_pp + i))], 0, 0)
        return pl.BlockSpec((1, page, e), idx)

    pages = [page_spec(i) for i in range(n_pp)]
    const = lambda shape: pl.BlockSpec(shape, lambda b, g, pt: (0,) * len(shape))
    out = pl.pallas_call(
        functools.partial(_sb_sample_kernel, n_pp=n_pp, page=page),
        grid_spec=pltpu.PrefetchScalarGridSpec(
            num_scalar_prefetch=1,
            grid=(s, n_pages // n_pp),
            in_specs=[pl.BlockSpec((1, h, e), lambda b, g, pt: (b, 0, 0)),
                      const((1, h)), const((page, page)), const((h, e))] + pages + pages,
            out_specs=pl.BlockSpec((1, 1, e), lambda b, g, pt: (b, 0, 0)),
            scratch_shapes=[pltpu.VMEM((1, h), F32), pltpu.VMEM((SUBLANES, e), F32)]),
        out_shape=jax.ShapeDtypeStruct((s, 1, e), F32),
        compiler_params=_params("parallel", "arbitrary"),
        name="sb_sample",
    )(page_table.reshape(-1), qbd, bias.reshape(1, h), later, expand,
      *([cache_k.reshape(n_phys, page, e)] * n_pp), *([cache_v.reshape(n_phys, page, e)] * n_pp))
    return out.reshape(s, e)


def kernel(x_prompt, x_sample, c_prompt, c_sample, state_ssm_re, state_ssm_im, cache_k, cache_v, page_table, g_pre_a, g_post_a, w_mod_a, b_mod_a, w_in_a, lam_re, lam_im, log_dt, bmat_re, bmat_im, cmat_re, cmat_im, d_skip, w_glu, b_glu, w_out_a, g_kv, w_mod_kv, b_mod_kv, w_kv, g_pre_b, g_post_b, w_mod_b, b_mod_b, w_in_b, sb_bias, w_out_b):
    n_b, t, d = x_prompt.shape
    n_s = x_sample.shape[0]
    n_a, n_bl = w_in_a.shape[0], w_in_b.shape[0]
    e = d
    q_scale = HEAD_DIM ** -0.5
    bf = lambda w: w.astype(BF16)

    c_all = jnp.concatenate([c_prompt, c_sample], axis=0)

    def mods(w_mod, b_mod):
        m = _ada_mod(c_all, bf(w_mod), b_mod)
        return m[:n_b].reshape(n_b, 1, -1), m[n_b:].reshape(1, n_s, -1)

    xp = x_prompt
    xs = x_sample.reshape(1, n_s, d)
    p_re, p_im, s_re, s_im = [], [], [], []
    uz_cfg = ((0, 2 * e, ((0, 1.0),)),)
    for i in range(n_a):
        mp, ms = mods(w_mod_a[i], b_mod_a[i])
        w_in, w_g, w_o = bf(w_in_a[i]), bf(w_glu[i]), bf(w_out_a[i])
        prompt_ops, sample_ops = _s5_operators(
            _s5_prep(lam_re[i], lam_im[i], log_dt[i], bmat_re[i], bmat_im[i], cmat_re[i], cmat_im[i]))
        (uz,) = _norm_mm(xp, mp, g_pre_a[i], w_in, uz_cfg, (F32,))
        y, hr, hi = _s5_scan(uz, prompt_ops)
        xp = _s5_post(y, uz, d_skip[i], w_g, b_glu[i], w_o, g_post_a[i], mp, xp)
        p_re.append(hr)
        p_im.append(hi)
        (uz,) = _norm_mm(xs, ms, g_pre_a[i], w_in, uz_cfg, (F32,))
        y, hr, hi = _s5_step(uz[0, :, :e], state_ssm_re[i], state_ssm_im[i], sample_ops)
        xs = _s5_post(y.reshape(1, n_s, e), uz, d_skip[i], w_g, b_glu[i], w_o, g_post_a[i], ms, xs)
        s_re.append(hr)
        s_im.append(hi)

    mp, ms = mods(w_mod_kv, b_mod_kv)
    w = bf(w_kv)
    kv_cfg = ((0, e, ((0, 1.0), (2, 1.0))), (e, e, ((1, 1.0), (3, 1.0))))
    p_k, p_v, pk_bf, pv_bf = _norm_mm(xp, mp, g_kv, w, kv_cfg, (F32, F32, BF16, BF16))
    kv_cfg_s = ((0, e, ((0, 1.0),)), (e, e, ((1, 1.0),)))
    s_k, s_v = _norm_mm(xs, ms, g_kv, w, kv_cfg_s, (F32, F32))

    qz_cfg = ((0, e, ((0, q_scale),)), (e, e, ((1, 1.0),)))
    qz_cfg_p = ((0, e, ((0, q_scale * LOG2E),)), (e, e, ((1, 1.0),)))
    for j in range(n_bl):
        mp, ms = mods(w_mod_b[j], b_mod_b[j])
        w_in, w_o = bf(w_in_b[j]), bf(w_out_b[j])
        q, z = _norm_mm(xp, mp, g_pre_b[j], w_in, qz_cfg_p, (BF16, F32))
        a = _sb_prompt(q, pk_bf, pv_bf, sb_bias[j])
        xp = _sb_post(a, z, w_o, g_post_b[j], mp, xp)
        q, z = _norm_mm(xs, ms, g_pre_b[j], w_in, qz_cfg, (F32, F32))
        a = _sb_sample(q[0], cache_k, cache_v, page_table, sb_bias[j])
        xs = _sb_post(a.reshape(1, n_s, e), z, w_o, g_post_b[j], ms, xs)

    return (xp, xs.reshape(n_s, 1, d), jnp.stack(p_re), jnp.stack(p_im),
            p_k.reshape(n_b, t, N_HEADS, HEAD_DIM), p_v.reshape(n_b, t, N_HEADS, HEAD_DIM),
            jnp.stack(s_re), jnp.stack(s_im),
            s_k.reshape(n_s, 1, N_HEADS, HEAD_DIM), s_v.reshape(n_s, 1, N_HEADS, HEAD_DIM))
```

```python
import functools

import jax
import jax.numpy as jnp
from jax import lax
from jax.experimental import pallas as pl
from jax.experimental.pallas import tpu as pltpu

F32 = jnp.float32
BF16 = jnp.bfloat16

D_MODEL = 1024
GROUP_CH = 16
N_GROUPS = D_MODEL // GROUP_CH
P_STATE = 64
N_HEADS = 16
HEAD_DIM = D_MODEL // N_HEADS
EPS = 1e-6

SSM_CHUNK = 8
S5_TIME_TILE = 512
GROUPS_PER_OCTET = 8
LANES = 128
SUBLANES = 8
ROW_TILE = 512
ATTN_TILE = 256
ATTN_HEAD_PAIRS = 2
LOG2E = 1.4426950408889634
PAGES_PER_STEP = 8
VMEM_LIMIT = 48 * 1024 * 1024

_NT = (((1,), (1,)), ((), ()))


def _params(*sem):
    return pltpu.CompilerParams(dimension_semantics=sem, vmem_limit_bytes=VMEM_LIMIT)


def _sigmoid(x):
    return 1.0 / (1.0 + jnp.exp(-x))


def _softplus(z):
    return jnp.maximum(z, 0.0) + jnp.log(1.0 + jnp.exp(-jnp.abs(z)))


def _softplus2(z):
    return jnp.maximum(z, 0.0) + jnp.log2(1.0 + jnp.exp2(-jnp.abs(z)))


def _gelu_tanh(x):
    return 0.5 * x * (1.0 + jnp.tanh(0.7978845608028654 * (x + 0.044715 * (x * x * x))))


def _split_bf16(x):
    hi = x.astype(BF16)
    lo = (x - hi.astype(F32)).astype(BF16)
    return hi, lo


def _dot(a, b):
    return jnp.dot(a, b, preferred_element_type=F32)


def _dot3(a, b, dims=None):
    a_hi, a_lo = _split_bf16(a)
    b_hi, b_lo = _split_bf16(b)
    if dims is None:
        f = _dot
    else:
        f = lambda p, q: lax.dot_general(p, q, dims, preferred_element_type=F32)
    return f(a_hi, b_hi) + (f(a_hi, b_lo) + f(a_lo, b_hi))


def _rms(x, g):
    return x * lax.rsqrt(jnp.mean(x * x, axis=-1, keepdims=True) + EPS) * g


def _mod_kernel(c_ref, w_ref, b_ref, o_ref):
    c = c_ref[...]
    s = (c * _sigmoid(c)).astype(BF16)
    o_ref[...] = _dot(s, w_ref[...]) + b_ref[...]


def _ada_mod(c, w, b):
    m, d = c.shape
    n = w.shape[1]
    tn = 1024
    return pl.pallas_call(
        _mod_kernel,
        grid=(n // tn,),
        in_specs=[pl.BlockSpec((m, d), lambda j: (0, 0)),
                  pl.BlockSpec((d, tn), lambda j: (0, j)),
                  pl.BlockSpec((1, tn), lambda j: (0, j))],
        out_specs=pl.BlockSpec((m, tn), lambda j: (0, j)),
        out_shape=jax.ShapeDtypeStruct((m, n), F32),
        compiler_params=_params("parallel"),
        name="ada_mod",
    )(c, w, b.reshape(1, n))


def _norm_mm_kernel(x_ref, sh_ref, sc_ref, g_ref, w_ref, *out_refs, cfg, nchunk):
    h = _rms(x_ref[0], g_ref[...]) * (1.0 + sc_ref[0]) + sh_ref[0]
    hb = h.astype(BF16)
    for col0, ncols, outs in cfg:
        for j in range(0, ncols, nchunk):
            r = _dot(hb, w_ref[:, col0 + j:col0 + j + nchunk])
            for oi, scale in outs:
                v = r if scale == 1.0 else r * scale
                out_refs[oi][0, :, j:j + nchunk] = v.astype(out_refs[oi].dtype)


def _norm_mm(x, m, g, w, cfg, out_dtypes):
    bx, t, d = x.shape
    tm = min(ROW_TILE, t)
    per_row = m.shape[1] != 1
    tmm = tm if per_row else 1
    mod_idx = (lambda col: (lambda b, i: (b, i, col))) if per_row else (lambda col: (lambda b, i: (b, 0, col)))
    widths = {}
    for _, ncols, outs in cfg:
        for oi, _ in outs:
            widths[oi] = ncols
    n_out = len(out_dtypes)
    return pl.pallas_call(
        functools.partial(_norm_mm_kernel, cfg=cfg, nchunk=512),
        grid=(bx, t // tm),
        in_specs=[pl.BlockSpec((1, tm, d), lambda b, i: (b, i, 0)),
                  pl.BlockSpec((1, tmm, d), mod_idx(0)),
                  pl.BlockSpec((1, tmm, d), mod_idx(1)),
                  pl.BlockSpec((1, d), lambda b, i: (0, 0)),
                  pl.BlockSpec(w.shape, lambda b, i: (0, 0))],
        out_specs=[pl.BlockSpec((1, tm, widths[oi]), lambda b, i: (b, i, 0)) for oi in range(n_out)],
        out_shape=[jax.ShapeDtypeStruct((bx, t, widths[oi]), out_dtypes[oi]) for oi in range(n_out)],
        compiler_params=_params("parallel", "parallel"),
        name="norm_proj",
    )(x, m, m, g.reshape(1, d), w)


def _residual_tail(y, w_out_ref, g_ref, gate_ref, x_ref, o_ref):
    o = _dot(y.astype(BF16), w_out_ref[...])
    o_ref[0] = x_ref[0] + gate_ref[0] * _rms(o, g_ref[...])


def _s5_post_kernel(y_ref, u_ref, z_ref, dsk_ref, wglu_ref, bglu_ref, wout_ref, g_ref, gate_ref,
                    x_ref, o_ref):
    y = _gelu_tanh(y_ref[0] + dsk_ref[...] * u_ref[0])
    y = y * _sigmoid(_dot(y.astype(BF16), wglu_ref[...]) + bglu_ref[...])
    z = z_ref[0]
    y = y * (z * _sigmoid(z))
    _residual_tail(y, wout_ref, g_ref, gate_ref, x_ref, o_ref)


def _sb_post_kernel(a_ref, z_ref, wout_ref, g_ref, gate_ref, x_ref, o_ref):
    z = z_ref[0]
    y = a_ref[0] * (z * _sigmoid(z))
    _residual_tail(y, wout_ref, g_ref, gate_ref, x_ref, o_ref)


def _row_specs(x, m):
    bx, t, d = x.shape
    tm = min(ROW_TILE, t)
    per_row = m.shape[1] != 1
    tmm = tm if per_row else 1
    gate_idx = (lambda b, i: (b, i, 2)) if per_row else (lambda b, i: (b, 0, 2))
    row = lambda col: pl.BlockSpec((1, tm, d), lambda b, i: (b, i, col))
    vec = pl.BlockSpec((1, d), lambda b, i: (0, 0))
    mat = pl.BlockSpec((d, d), lambda b, i: (0, 0))
    gate = pl.BlockSpec((1, tmm, d), gate_idx)
    return (bx, t // tm), row, vec, mat, gate


def _s5_post(y, uz, d_skip, w_glu, b_glu, w_out, g_post, m, x):
    grid, row, vec, mat, gate = _row_specs(x, m)
    d = x.shape[-1]
    return pl.pallas_call(
        _s5_post_kernel,
        grid=grid,
        in_specs=[row(0), row(0), row(1), vec, mat, vec, mat, vec, gate, row(0)],
        out_specs=row(0),
        out_shape=jax.ShapeDtypeStruct(x.shape, F32),
        compiler_params=_params("parallel", "parallel"),
        name="s5_post",
    )(y, uz, uz, d_skip.reshape(1, d), w_glu, b_glu.reshape(1, d), w_out, g_post.reshape(1, d), m, x)


def _sb_post(a, z, w_out, g_post, m, x):
    grid, row, vec, mat, gate = _row_specs(x, m)
    d = x.shape[-1]
    return pl.pallas_call(
        _sb_post_kernel,
        grid=grid,
        in_specs=[row(0), row(0), mat, vec, gate, row(0)],
        out_specs=row(0),
        out_shape=jax.ShapeDtypeStruct(x.shape, F32),
        compiler_params=_params("parallel", "parallel"),
        name="sb_post",
    )(a, z, w_out, g_post.reshape(1, d), m, x)


def _s5_prep_kernel(lre_ref, lim_ref, ldt_ref, btr_ref, bti_ref, cre_ref, cim_ref,
                    a_ref, al_ref, bbr_ref, bbi_ref, wsr_ref, wsi_ref, catr_ref, catn_ref, kft_ref):
    L = SSM_CHUNK
    lr, li = lre_ref[0], lim_ref[0]
    dt = jnp.exp(ldt_ref[0])
    mag = jnp.exp(lr * dt)
    ar, ai = mag * jnp.cos(li * dt), mag * jnp.sin(li * dt)
    den = lr * lr + li * li
    cr = ((ar - 1.0) * lr + ai * li) / den
    ci = (ai * lr - (ar - 1.0) * li) / den
    btr, bti = btr_ref[0], bti_ref[0]
    bbr, bbi = cr * btr - ci * bti, cr * bti + ci * btr
    cre, cim = cre_ref[0], cim_ref[0]
    a_ref[0, 0:1, :] = ar
    a_ref[0, 1:2, :] = ai
    bbr_ref[0] = bbr
    bbi_ref[0] = bbi
    pr, pi = jnp.ones_like(ar), jnp.zeros_like(ar)
    for t in range(L + 1):
        rows = slice(t * GROUP_CH, (t + 1) * GROUP_CH)
        catr_ref[0, rows, :] = cre * pr - cim * pi
        catn_ref[0, rows, :] = -(cre * pi + cim * pr)
        if t < L:
            wrows = slice((L - 1 - t) * GROUP_CH, (L - t) * GROUP_CH)
            wsr_ref[0, wrows, :] = pr * bbr - pi * bbi
            wsi_ref[0, wrows, :] = pr * bbi + pi * bbr
            pr, pi = pr * ar - pi * ai, pr * ai + pi * ar
    al_ref[0, 0:1, :] = pr
    al_ref[0, 1:2, :] = pi
    kft_ref[0] = _dot3(catr_ref[0], bbr, _NT) + _dot3(catn_ref[0], bbi, _NT)


def _s5_prep(lam_re, lam_im, log_dt, bmat_re, bmat_im, cmat_re, cmat_im):
    g, p, gc, L = N_GROUPS, P_STATE, GROUP_CH, SSM_CHUNK
    vec = pl.BlockSpec((1, 1, p), lambda i: (i, 0, 0))
    mat = pl.BlockSpec((1, gc, p), lambda i: (i, 0, 0))
    blk = lambda r, c: pl.BlockSpec((1, r, c), lambda i: (i, 0, 0))
    shp = lambda r, c: jax.ShapeDtypeStruct((g, r, c), F32)
    return pl.pallas_call(
        _s5_prep_kernel,
        grid=(g,),
        in_specs=[vec, vec, pl.BlockSpec((1, 1, 1), lambda i: (i, 0, 0)), mat, mat, mat, mat],
        out_specs=[blk(2, p), blk(2, p), blk(gc, p), blk(gc, p), blk(L * gc, p), blk(L * gc, p),
                   blk((L + 1) * gc, p), blk((L + 1) * gc, p), blk((L + 1) * gc, gc)],
        out_shape=[shp(2, p), shp(2, p), shp(gc, p), shp(gc, p), shp(L * gc, p), shp(L * gc, p),
                   shp((L + 1) * gc, p), shp((L + 1) * gc, p), shp((L + 1) * gc, gc)],
        compiler_params=_params("parallel"),
        name="s5_prep",
    )(lam_re.reshape(g, 1, p), lam_im.reshape(g, 1, p), log_dt.reshape(g, 1, 1),
      jnp.swapaxes(bmat_re, 1, 2), jnp.swapaxes(bmat_im, 1, 2), cmat_re, cmat_im)


def _s5_operators(prep):
    a, al, bbr, bbi, wsr, wsi, catr, catn, kft = prep
    g, p, gc, L = N_GROUPS, P_STATE, GROUP_CH, SSM_CHUNK
    no, go = g // GROUPS_PER_OCTET, GROUPS_PER_OCTET
    eye = jnp.eye(go, dtype=F32)
    eye_b = eye.astype(BF16)
    bd = lambda x, spec, rows: jnp.einsum(
        spec, x.astype(BF16).reshape(no, go, L, gc, -1), eye_b).reshape(no, rows, -1)
    k_lag = bd(kft[:, :L * gc], 'ogtcd,gh->otgdhc', L * go * gc)
    w_state = jnp.concatenate([bd(wsr, 'oglcp,gh->olgchp', L * go * gc),
                               bd(wsi, 'oglcp,gh->olgchp', L * go * gc)], axis=-1)
    w_carry = jnp.concatenate([bd(catr[:, gc:], 'oglcp,gh->ogplhc', go * p),
                               bd(catn[:, gc:], 'oglcp,gh->ogplhc', go * p)], axis=1)
    prompt_ops = (k_lag.astype(BF16), w_state.astype(BF16), w_carry.astype(BF16),
                  al[:, 0].reshape(1, g * p), al[:, 1].reshape(1, g * p))

    def in_bd(bt):
        return jnp.einsum('ogcp,gh->ogchp', bt.reshape(no, go, gc, p), eye).reshape(no, go * gc, go * p)

    def out_bd(ct):
        return jnp.einsum('ogcp,gh->ogphc', ct.reshape(no, go, gc, p), eye).reshape(no, go * p, go * gc)

    sample_ops = (a[:, 0].reshape(1, g * p), a[:, 1].reshape(1, g * p), in_bd(bbr), in_bd(bbi),
                  out_bd(catr[:, :gc]), out_bd(catn[:, :gc]))
    return prompt_ops, sample_ops


def _s5_scan_kernel(u_ref, klag_ref, wst_ref, wcar_ref, alr_ref, ali_ref, y_ref, hr_ref, hi_ref,
                    lagged, chunk_in, s_re, s_im, hp_re, hp_im, from_state, st_re, st_im, *, n_b, tt):
    L, W = SSM_CHUNK, LANES
    nc = tt // L
    hw = st_re.shape[-1]

    @pl.when(pl.program_id(1) == 0)
    def _():
        st_re[...] = jnp.zeros_like(st_re)
        st_im[...] = jnp.zeros_like(st_im)

    sub = lax.broadcasted_iota(jnp.int32, (nc, L, W), 1)
    for b in range(n_b):
        x3 = u_ref[b].reshape(nc, L, W)
        for lag in range(L):
            shifted = x3 if lag == 0 else jnp.where(sub >= lag, pltpu.roll(x3, lag, axis=1), 0.0)
            lagged[:, lag * W:(lag + 1) * W] = shifted.reshape(tt, W).astype(BF16)
        y_ref[b] = _dot(lagged[...], klag_ref[0])
        for l in range(L):
            chunk_in[l, pl.ds(b, nc, stride=n_b), :] = u_ref[b, pl.ds(l, nc, stride=L), :]

    s = _dot(jnp.concatenate([chunk_in[l].astype(BF16) for l in range(L)], axis=1), wst_ref[0])
    s_re[...] = s[:, :hw]
    s_im[...] = s[:, hw:]
    alr = jnp.broadcast_to(alr_ref[...], (n_b, hw))
    ali = jnp.broadcast_to(ali_ref[...], (n_b, hw))

    def step(k, carry):
        hr, hi = carry
        rows = pl.ds(pl.multiple_of(k * n_b, n_b), n_b)
        hp_re[rows, :] = hr
        hp_im[rows, :] = hi
        return (alr * hr - ali * hi + s_re[rows, :], alr * hi + ali * hr + s_im[rows, :])

    hr, hi = lax.fori_loop(0, nc, step, (st_re[...], st_im[...]))
    st_re[...] = hr
    st_im[...] = hi
    hr_ref[...] = hr
    hi_ref[...] = hi
    r = (_dot(hp_re[...].astype(BF16), wcar_ref[0, :hw, :])
         + _dot(hp_im[...].astype(BF16), wcar_ref[0, hw:, :]))
    for l in range(L):
        from_state[l] = r[:, l * W:(l + 1) * W]
    for b in range(n_b):
        for l in range(L):
            y_ref[b, pl.ds(l, nc, stride=L), :] += from_state[l, pl.ds(b, nc, stride=n_b), :]


def _s5_scan(uz, ops):
    k_lag, w_state, w_carry, alr, ali = ops
    n_b, t, e2 = uz.shape
    e = e2 // 2
    g, p, L, W = N_GROUPS, P_STATE, SSM_CHUNK, LANES
    no = e // W
    hw = GROUPS_PER_OCTET * p
    tt = min(S5_TIME_TILE, t)
    assert n_b == SUBLANES and L == SUBLANES and t % tt == 0 and GROUPS_PER_OCTET * GROUP_CH == W
    rows = n_b * tt // L
    per_octet = lambda r, c: pl.BlockSpec((1, r, c), lambda o, i: (o, 0, 0))
    state_vec = pl.BlockSpec((1, hw), lambda o, i: (0, o))
    state_out = pl.BlockSpec((n_b, hw), lambda o, i: (0, o))
    io = pl.BlockSpec((n_b, tt, W), lambda o, i: (0, i, o))
    y, hr, hi = pl.pallas_call(
        functools.partial(_s5_scan_kernel, n_b=n_b, tt=tt),
        grid=(no, t // tt),
        in_specs=[io, per_octet(L * W, W), per_octet(L * W, 2 * hw), per_octet(2 * hw, L * W),
                  state_vec, state_vec],
        out_specs=[io, state_out, state_out],
        out_shape=[jax.ShapeDtypeStruct((n_b, t, e), F32),
                   jax.ShapeDtypeStruct((n_b, g * p), F32),
                   jax.ShapeDtypeStruct((n_b, g * p), F32)],
        scratch_shapes=[pltpu.VMEM((tt, L * W), BF16), pltpu.VMEM((L, rows, W), F32),
                        pltpu.VMEM((rows, hw), F32), pltpu.VMEM((rows, hw), F32),
                        pltpu.VMEM((rows, hw), F32), pltpu.VMEM((rows, hw), F32),
                        pltpu.VMEM((L, rows, W), F32),
                        pltpu.VMEM((n_b, hw), F32), pltpu.VMEM((n_b, hw), F32)],
        compiler_params=_params("parallel", "arbitrary"),
        name="s5_scan",
    )(uz, k_lag, w_state, w_carry, alr, ali)
    return y, hr.reshape(n_b, g, p), hi.reshape(n_b, g, p)


def _s5_step_kernel(u_ref, h0r_ref, h0i_ref, ar_ref, ai_ref, bbr_ref, bbi_ref, cr_ref, cn_ref,
                    y_ref, hr_ref, hi_ref):
    u = u_ref[...]
    ar, ai = ar_ref[...], ai_ref[...]
    h0r, h0i = h0r_ref[...], h0i_ref[...]
    hr = ar * h0r - ai * h0i + _dot3(u, bbr_ref[0])
    hi = ar * h0i + ai * h0r + _dot3(u, bbi_ref[0])
    hr_ref[...] = hr
    hi_ref[...] = hi
    y_ref[...] = _dot3(hr, cr_ref[0]) + _dot3(hi, cn_ref[0])


def _s5_step(u, h0_re, h0_im, ops):
    ar, ai, bbr, bbi, cr, cn = ops
    s, e = u.shape
    g, p = N_GROUPS, P_STATE
    no = g // GROUPS_PER_OCTET
    wc, ws = GROUPS_PER_OCTET * GROUP_CH, GROUPS_PER_OCTET * p
    assert wc == LANES
    ch = pl.BlockSpec((s, wc), lambda o: (0, o))
    st = pl.BlockSpec((s, ws), lambda o: (0, o))
    av = pl.BlockSpec((1, ws), lambda o: (0, o))
    w_in = pl.BlockSpec((1, wc, ws), lambda o: (o, 0, 0))
    w_out = pl.BlockSpec((1, ws, wc), lambda o: (o, 0, 0))
    y, hr, hi = pl.pallas_call(
        _s5_step_kernel,
        grid=(no,),
        in_specs=[ch, st, st, av, av, w_in, w_in, w_out, w_out],
        out_specs=[ch, st, st],
        out_shape=[jax.ShapeDtypeStruct((s, e), F32),
                   jax.ShapeDtypeStruct((s, g * p), F32),
                   jax.ShapeDtypeStruct((s, g * p), F32)],
        compiler_params=_params("parallel"),
        name="s5_step",
    )(u, h0_re.reshape(s, g * p), h0_im.reshape(s, g * p), ar, ai, bbr, bbi, cr, cn)
    return y, hr.reshape(s, g, p), hi.reshape(s, g, p)


def _sb_prompt_kernel(bias_ref, q_ref, k_ref, v_ref, u_ref, o_ref, acc, carry, w_s, z_s, *, tile, pairs):
    hg, qi = pl.program_id(1), pl.program_id(2)
    heads = range(2 * pairs)
    lanes = [slice((h // 2) * LANES, (h // 2 + 1) * LANES) for h in heads]
    first = lax.broadcasted_iota(jnp.int32, (tile, LANES), 1) < HEAD_DIM
    qm, bias = [], []
    for p in range(pairs):
        q2 = q_ref[0, :, p * LANES:(p + 1) * LANES]
        zero = jnp.zeros_like(q2)
        qm += [jnp.where(first, q2, zero), jnp.where(first, zero, q2)]
        bias += [bias_ref[2 * (hg * pairs + p)] * LOG2E, bias_ref[2 * (hg * pairs + p) + 1] * LOG2E]
    u = u_ref[...]
    row = lax.broadcasted_iota(jnp.int32, (tile, tile), 0)
    col = lax.broadcasted_iota(jnp.int32, (tile, tile), 1)
    below = col < row

    def block(j):
        return pl.ds(pl.multiple_of(j * tile, tile), tile)

    def weigh_values(j):
        for h in heads:
            acc[h] += _dot(w_s[h], v_ref[0, block(j), lanes[h]])

    def scores(j):
        return [lax.dot_general(qm[h], k_ref[0, block(j), lanes[h]], _NT, preferred_element_type=F32) + bias[h]
                for h in heads]

    def visit(j, diagonal):
        z = [z_s[h] for h in heads]
        z_next = scores(jnp.maximum(j - 1, 0))
        if not diagonal:
            weigh_values(j + 1)
        sp = [_softplus2(z[h]) for h in heads]
        if diagonal:
            sp = [jnp.where(below, sp[h], 0.0) for h in heads]
        cum = [_dot(sp[h].astype(BF16), u) for h in heads]
        total = [jnp.broadcast_to(cum[h][:, 0:1], (tile, LANES)) for h in heads]
        if diagonal:
            w = [jnp.where(below, jnp.exp2(z[h] + cum[h]), 0.0) for h in heads]
        else:
            w = [jnp.exp2(z[h] + (cum[h] + jnp.concatenate([carry[h]] * (tile // LANES), axis=1)))
                 for h in heads]
        for h in heads:
            w_s[h] = w[h].astype(BF16)
            z_s[h] = z_next[h]
            if diagonal:
                carry[h] = total[h]
            else:
                carry[h] += total[h]

    acc[...] = jnp.zeros_like(acc)
    for h, z0 in enumerate(scores(qi)):
        z_s[h] = z0
    visit(qi, True)

    def body(t, c):
        visit(qi - 1 - t, False)
        return c

    lax.fori_loop(0, qi, body, 0)
    weigh_values(0)
    for p in range(pairs):
        o_ref[0, :, p * LANES:(p + 1) * LANES] = jnp.where(first, acc[2 * p], acc[2 * p + 1])


def _sb_prompt(q, k, v, bias):
    n_b, t, e = q.shape
    tile = min(ATTN_TILE, t)
    pairs = ATTN_HEAD_PAIRS
    width = pairs * LANES
    assert 2 * HEAD_DIM == LANES and t % tile == 0 and tile % LANES == 0 and e % width == 0
    not_before = jnp.arange(tile)[:, None] >= jnp.arange(tile)[None, :]
    u = -not_before.astype(BF16)
    kv = pl.BlockSpec((1, t, width), lambda b, h, i: (b, 0, h))
    qo = pl.BlockSpec((1, tile, width), lambda b, h, i: (b, i, h))
    return pl.pallas_call(
        functools.partial(_sb_prompt_kernel, tile=tile, pairs=pairs),
        grid=(n_b, e // width, t // tile),
        in_specs=[pl.BlockSpec(memory_space=pltpu.SMEM), qo, kv, kv,
                  pl.BlockSpec((tile, tile), lambda b, h, i: (0, 0))],
        out_specs=qo,
        out_shape=jax.ShapeDtypeStruct((n_b, t, e), F32),
        scratch_shapes=[pltpu.VMEM((2 * pairs, tile, LANES), F32),
                        pltpu.VMEM((2 * pairs, tile, LANES), F32),
                        pltpu.VMEM((2 * pairs, tile, tile), BF16),
                        pltpu.VMEM((2 * pairs, tile, tile), F32)],
        compiler_params=_params("parallel", "parallel", "parallel"),
        name="sb_prompt",
    )(bias, q, k, v, u)


def _sb_sample_kernel(pt_ref, q_ref, bias_ref, u_ref, *refs, n_pp):
    k_refs, v_refs = refs[:n_pp], refs[n_pp:2 * n_pp]
    o_ref, carry, acc = refs[2 * n_pp:]
    step = pl.program_id(1)

    @pl.when(step == 0)
    def _():
        carry[...] = jnp.zeros_like(carry)
        acc[...] = jnp.zeros_like(acc)

    q = q_ref[0]
    u = u_ref[...]
    pages = range(n_pp)
    z = [jnp.sum(k_refs[i][0] * q, axis=1) + bias_ref[...] for i in pages]
    sp = [_softplus(z[i]) for i in pages]
    parts = [_split_bf16(sp[i]) for i in pages]
    cum = [_dot(parts[i][0], u) + _dot(parts[i][1], u) for i in pages]
    run = carry[...]
    w = []
    for i in pages:
        w.append(jnp.exp(z[i] - sp[i] + (cum[i] + run)))
        run = run + jnp.broadcast_to(cum[i][:, 0:1] - sp[i][:, 0:1], run.shape)
    carry[...] = run
    part = acc[...]
    for i in pages:
        part = part + v_refs[i][0] * w[i][:, None, :]
    acc[...] = part

    @pl.when(step == pl.num_programs(1) - 1)
    def _():
        o_ref[0] = jnp.sum(acc[...], axis=2)


def _sb_sample(q, cache_k, cache_v, page_table, bias):
    s, e = q.shape
    n_phys, page, h, dh = cache_k.shape
    n_pages = page_table.shape[1]
    n_pp = PAGES_PER_STEP if n_pages % PAGES_PER_STEP == 0 else 1
    assert page == LANES
    k_t = cache_k.transpose(0, 2, 3, 1)
    v_t = cache_v.transpose(0, 2, 3, 1)
    q_lanes = jnp.broadcast_to(q.reshape(s, h, dh, 1), (s, h, dh, page))
    bias_lanes = jnp.broadcast_to(bias.reshape(h, 1), (h, page))
    t1 = jnp.arange(page)
    u = -(t1[:, None] > t1[None, :]).astype(BF16)

    def page_spec(i):
        def idx(b, g, pt):
            return (pt[b * n_pages + (n_pages - 1 - (g * n_pp + i))], 0, 0, 0)
        return pl.BlockSpec((1, h, dh, page), idx)

    pages = [page_spec(i) for i in range(n_pp)]
    const = lambda shape: pl.BlockSpec(shape, lambda b, g, pt: (0,) * len(shape))
    out = pl.pallas_call(
        functools.partial(_sb_sample_kernel, n_pp=n_pp),
        grid_spec=pltpu.PrefetchScalarGridSpec(
            num_scalar_prefetch=1,
            grid=(s, n_pages // n_pp),
            in_specs=[pl.BlockSpec((1, h, dh, page), lambda b, g, pt: (b, 0, 0, 0)),
                      const((h, page)), const((page, page))] + pages + pages,
            out_specs=pl.BlockSpec((1, h, dh), lambda b, g, pt: (b, 0, 0)),
            scratch_shapes=[pltpu.VMEM((h, page), F32), pltpu.VMEM((h, dh, page), F32)]),
        out_shape=jax.ShapeDtypeStruct((s, h, dh), F32),
        compiler_params=_params("parallel", "arbitrary"),
        name="sb_sample",
    )(page_table.reshape(-1), q_lanes, bias_lanes, u, *([k_t] * n_pp), *([v_t] * n_pp))
    return out.reshape(s, e)


def kernel(x_prompt, x_sample, c_prompt, c_sample, state_ssm_re, state_ssm_im, cache_k, cache_v, page_table, g_pre_a, g_post_a, w_mod_a, b_mod_a, w_in_a, lam_re, lam_im, log_dt, bmat_re, bmat_im, cmat_re, cmat_im, d_skip, w_glu, b_glu, w_out_a, g_kv, w_mod_kv, b_mod_kv, w_kv, g_pre_b, g_post_b, w_mod_b, b_mod_b, w_in_b, sb_bias, w_out_b):
    n_b, t, d = x_prompt.shape
    n_s = x_sample.shape[0]
    n_a, n_bl = w_in_a.shape[0], w_in_b.shape[0]
    e = d
    q_scale = HEAD_DIM ** -0.5
    bf = lambda w: w.astype(BF16)

    c_all = jnp.concatenate([c_prompt, c_sample], axis=0)

    def mods(w_mod, b_mod):
        m = _ada_mod(c_all, bf(w_mod), b_mod)
        return m[:n_b].reshape(n_b, 1, -1), m[n_b:].reshape(1, n_s, -1)

    xp = x_prompt
    xs = x_sample.reshape(1, n_s, d)
    p_re, p_im, s_re, s_im = [], [], [], []
    uz_cfg = ((0, 2 * e, ((0, 1.0),)),)
    for i in range(n_a):
        mp, ms = mods(w_mod_a[i], b_mod_a[i])
        w_in, w_g, w_o = bf(w_in_a[i]), bf(w_glu[i]), bf(w_out_a[i])
        prompt_ops, sample_ops = _s5_operators(
            _s5_prep(lam_re[i], lam_im[i], log_dt[i], bmat_re[i], bmat_im[i], cmat_re[i], cmat_im[i]))
        (uz,) = _norm_mm(xp, mp, g_pre_a[i], w_in, uz_cfg, (F32,))
        y, hr, hi = _s5_scan(uz, prompt_ops)
        xp = _s5_post(y, uz, d_skip[i], w_g, b_glu[i], w_o, g_post_a[i], mp, xp)
        p_re.append(hr)
        p_im.append(hi)
        (uz,) = _norm_mm(xs, ms, g_pre_a[i], w_in, uz_cfg, (F32,))
        y, hr, hi = _s5_step(uz[0, :, :e], state_ssm_re[i], state_ssm_im[i], sample_ops)
        xs = _s5_post(y.reshape(1, n_s, e), uz, d_skip[i], w_g, b_glu[i], w_o, g_post_a[i], ms, xs)
        s_re.append(hr)
        s_im.append(hi)

    mp, ms = mods(w_mod_kv, b_mod_kv)
    w = bf(w_kv)
    kv_cfg = ((0, e, ((0, 1.0), (2, 1.0))), (e, e, ((1, 1.0), (3, 1.0))))
    p_k, p_v, pk_bf, pv_bf = _norm_mm(xp, mp, g_kv, w, kv_cfg, (F32, F32, BF16, BF16))
    kv_cfg_s = ((0, e, ((0, 1.0),)), (e, e, ((1, 1.0),)))
    s_k, s_v = _norm_mm(xs, ms, g_kv, w, kv_cfg_s, (F32, F32))

    qz_cfg = ((0, e, ((0, q_scale),)), (e, e, ((1, 1.0),)))
    qz_cfg_p = ((0, e, ((0, q_scale * LOG2E),)), (e, e, ((1, 1.0),)))
    for j in range(n_bl):
        mp, ms = mods(w_mod_b[j], b_mod_b[j])
        w_in, w_o = bf(w_in_b[j]), bf(w_out_b[j])
        q, z = _norm_mm(xp, mp, g_pre_b[j], w_in, qz_cfg_p, (BF16, F32))
        a = _sb_prompt(q, pk_bf, pv_bf, sb_bias[j])
        xp = _sb_post(a, z, w_o, g_post_b[j], mp, xp)
        q, z = _norm_mm(xs, ms, g_pre_b[j], w_in, qz_cfg, (F32, F32))
        a = _sb_sample(q[0], cache_k, cache_v, page_table, sb_bias[j])
        xs = _sb_post(a.reshape(1, n_s, e), z, w_o, g_post_b[j], ms, xs)

    return (xp, xs.reshape(n_s, 1, d), jnp.stack(p_re), jnp.stack(p_im),
            p_k.reshape(n_b, t, N_HEADS, HEAD_DIM), p_v.reshape(n_b, t, N_HEADS, HEAD_DIM),
            jnp.stack(s_re), jnp.stack(s_im),
            s_k.reshape(n_s, 1, N_HEADS, HEAD_DIM), s_v.reshape(n_s, 1, N_HEADS, HEAD_DIM))
```

```python
import functools

import jax
import jax.numpy as jnp
from jax import lax
from jax.experimental import pallas as pl
from jax.experimental.pallas import tpu as pltpu

F32 = jnp.float32
BF16 = jnp.bfloat16

D_MODEL = 1024
GROUP_CH = 16
N_GROUPS = D_MODEL // GROUP_CH
P_STATE = 64
N_HEADS = 16
HEAD_DIM = D_MODEL // N_HEADS
EPS = 1e-6

SSM_CHUNK = 8
S5_TIME_TILE = 512
GROUPS_PER_OCTET = 8
LANES = 128
SUBLANES = 8
ROW_TILE = 512
ATTN_TILE = 256
ATTN_HEAD_PAIRS = 4
LOG2E = 1.4426950408889634
PAGES_PER_STEP = 8
VMEM_LIMIT = 48 * 1024 * 1024

_NT = (((1,), (1,)), ((), ()))


def _params(*sem):
    return pltpu.CompilerParams(dimension_semantics=sem, vmem_limit_bytes=VMEM_LIMIT)


def _sigmoid(x):
    return 0.5 * jnp.tanh(0.5 * x) + 0.5


def _softplus(z):
    return jnp.maximum(z, 0.0) + jnp.log(1.0 + jnp.exp(-jnp.abs(z)))


def _softplus2(z):
    return jnp.maximum(z, 0.0) + jnp.log2(1.0 + jnp.exp2(-jnp.abs(z)))


_GELU_C1 = 0.7978845608028654
_GELU_C3 = _GELU_C1 * 0.044715


def _gelu_tanh(x):
    half = 0.5 * x
    return half + half * jnp.tanh(x * (_GELU_C1 + _GELU_C3 * (x * x)))


def _split_bf16(x):
    hi = x.astype(BF16)
    lo = (x - hi.astype(F32)).astype(BF16)
    return hi, lo


def _dot(a, b):
    return jnp.dot(a, b, preferred_element_type=F32)


def _dot3(a, b, dims=None):
    a_hi, a_lo = _split_bf16(a)
    b_hi, b_lo = _split_bf16(b)
    if dims is None:
        f = _dot
    else:
        f = lambda p, q: lax.dot_general(p, q, dims, preferred_element_type=F32)
    return f(a_hi, b_hi) + (f(a_hi, b_lo) + f(a_lo, b_hi))


def _rms(x, g):
    return x * lax.rsqrt(jnp.mean(x * x, axis=-1, keepdims=True) + EPS) * g


def _mod_kernel(c_ref, w_ref, b_ref, o_ref):
    c = c_ref[...]
    s = (c * _sigmoid(c)).astype(BF16)
    o_ref[...] = _dot(s, w_ref[...].astype(BF16)) + b_ref[...]


def _ada_mod(c, w, b):
    m, d = c.shape
    n = w.shape[1]
    tn = 1024
    return pl.pallas_call(
        _mod_kernel,
        grid=(n // tn,),
        in_specs=[pl.BlockSpec((m, d), lambda j: (0, 0)),
                  pl.BlockSpec((d, tn), lambda j: (0, j)),
                  pl.BlockSpec((1, tn), lambda j: (0, j))],
        out_specs=pl.BlockSpec((m, tn), lambda j: (0, j)),
        out_shape=jax.ShapeDtypeStruct((m, n), F32),
        compiler_params=_params("parallel"),
        name="ada_mod",
    )(c, w, b.reshape(1, n))


def _norm_mm_kernel(x_ref, sh_ref, sc_ref, g_ref, w_ref, *out_refs, cfg, nchunk):
    h = _rms(x_ref[0], g_ref[...]) * (1.0 + sc_ref[0]) + sh_ref[0]
    hb = h.astype(BF16)
    for col0, ncols, outs in cfg:
        for j in range(0, ncols, nchunk):
            r = _dot(hb, w_ref[:, col0 + j:col0 + j + nchunk])
            for oi, scale in outs:
                v = r if scale == 1.0 else r * scale
                out_refs[oi][0, :, j:j + nchunk] = v.astype(out_refs[oi].dtype)


def _norm_mm(x, m, g, w, cfg, out_dtypes):
    bx, t, d = x.shape
    tm = min(ROW_TILE, t)
    per_row = m.shape[1] != 1
    tmm = tm if per_row else 1
    mod_idx = (lambda col: (lambda b, i: (b, i, col))) if per_row else (lambda col: (lambda b, i: (b, 0, col)))
    widths = {}
    for _, ncols, outs in cfg:
        for oi, _ in outs:
            widths[oi] = ncols
    n_out = len(out_dtypes)
    return pl.pallas_call(
        functools.partial(_norm_mm_kernel, cfg=cfg, nchunk=512),
        grid=(bx, t // tm),
        in_specs=[pl.BlockSpec((1, tm, d), lambda b, i: (b, i, 0)),
                  pl.BlockSpec((1, tmm, d), mod_idx(0)),
                  pl.BlockSpec((1, tmm, d), mod_idx(1)),
                  pl.BlockSpec((1, d), lambda b, i: (0, 0)),
                  pl.BlockSpec(w.shape, lambda b, i: (0, 0))],
        out_specs=[pl.BlockSpec((1, tm, widths[oi]), lambda b, i: (b, i, 0)) for oi in range(n_out)],
        out_shape=[jax.ShapeDtypeStruct((bx, t, widths[oi]), out_dtypes[oi]) for oi in range(n_out)],
        compiler_params=_params("parallel", "parallel"),
        name="norm_proj",
    )(x, m, m, g.reshape(1, d), w)


def _residual_tail(y, w_out_ref, g_ref, gate_ref, x_ref, o_ref):
    o = _dot(y.astype(BF16), w_out_ref[...])
    o_ref[0] = x_ref[0] + gate_ref[0] * _rms(o, g_ref[...])


def _s5_post_kernel(y_ref, u_ref, z_ref, dsk_ref, wglu_ref, bglu_ref, wout_ref, g_ref, gate_ref,
                    x_ref, o_ref):
    y = _gelu_tanh(y_ref[0] + dsk_ref[...] * u_ref[0])
    y = y * _sigmoid(_dot(y.astype(BF16), wglu_ref[...]) + bglu_ref[...])
    z = z_ref[0].astype(F32)
    y = y * (z * _sigmoid(z))
    _residual_tail(y, wout_ref, g_ref, gate_ref, x_ref, o_ref)


def _sb_post_kernel(a_ref, z_ref, wout_ref, g_ref, gate_ref, x_ref, o_ref):
    z = z_ref[0].astype(F32)
    y = a_ref[0].astype(F32) * (z * _sigmoid(z))
    _residual_tail(y, wout_ref, g_ref, gate_ref, x_ref, o_ref)


def _row_specs(x, m):
    bx, t, d = x.shape
    tm = min(ROW_TILE, t)
    per_row = m.shape[1] != 1
    tmm = tm if per_row else 1
    gate_idx = (lambda b, i: (b, i, 2)) if per_row else (lambda b, i: (b, 0, 2))
    row = lambda col: pl.BlockSpec((1, tm, d), lambda b, i: (b, i, col))
    vec = pl.BlockSpec((1, d), lambda b, i: (0, 0))
    mat = pl.BlockSpec((d, d), lambda b, i: (0, 0))
    gate = pl.BlockSpec((1, tmm, d), gate_idx)
    return (bx, t // tm), row, vec, mat, gate


def _s5_post(y, u, z, d_skip, w_glu, b_glu, w_out, g_post, m, x):
    grid, row, vec, mat, gate = _row_specs(x, m)
    d = x.shape[-1]
    return pl.pallas_call(
        _s5_post_kernel,
        grid=grid,
        in_specs=[row(0), row(0), row(0), vec, mat, vec, mat, vec, gate, row(0)],
        out_specs=row(0),
        out_shape=jax.ShapeDtypeStruct(x.shape, F32),
        compiler_params=_params("parallel", "parallel"),
        name="s5_post",
    )(y, u, z, d_skip.reshape(1, d), w_glu, b_glu.reshape(1, d), w_out, g_post.reshape(1, d), m, x)


def _sb_post(a, z, w_out, g_post, m, x):
    grid, row, vec, mat, gate = _row_specs(x, m)
    d = x.shape[-1]
    return pl.pallas_call(
        _sb_post_kernel,
        grid=grid,
        in_specs=[row(0), row(0), mat, vec, gate, row(0)],
        out_specs=row(0),
        out_shape=jax.ShapeDtypeStruct(x.shape, F32),
        compiler_params=_params("parallel", "parallel"),
        name="sb_post",
    )(a, z, w_out, g_post.reshape(1, d), m, x)


def _s5_prep_kernel(lre_ref, lim_ref, ldt_ref, btr_ref, bti_ref, cre_ref, cim_ref,
                    a_ref, al_ref, bbr_ref, bbi_ref, wsr_ref, wsi_ref, catr_ref, catn_ref, kft_ref):
    L = SSM_CHUNK
    lr, li = lre_ref[0], lim_ref[0]
    dt = jnp.exp(ldt_ref[0])
    mag = jnp.exp(lr * dt)
    ar, ai = mag * jnp.cos(li * dt), mag * jnp.sin(li * dt)
    den = lr * lr + li * li
    cr = ((ar - 1.0) * lr + ai * li) / den
    ci = (ai * lr - (ar - 1.0) * li) / den
    btr, bti = btr_ref[0], bti_ref[0]
    bbr, bbi = cr * btr - ci * bti, cr * bti + ci * btr
    cre, cim = cre_ref[0], cim_ref[0]
    a_ref[0, 0:1, :] = ar
    a_ref[0, 1:2, :] = ai
    bbr_ref[0] = bbr
    bbi_ref[0] = bbi
    pr, pi = jnp.ones_like(ar), jnp.zeros_like(ar)
    for t in range(L + 1):
        rows = slice(t * GROUP_CH, (t + 1) * GROUP_CH)
        catr_ref[0, rows, :] = cre * pr - cim * pi
        catn_ref[0, rows, :] = -(cre * pi + cim * pr)
        if t < L:
            wrows = slice((L - 1 - t) * GROUP_CH, (L - t) * GROUP_CH)
            wsr_ref[0, wrows, :] = pr * bbr - pi * bbi
            wsi_ref[0, wrows, :] = pr * bbi + pi * bbr
            pr, pi = pr * ar - pi * ai, pr * ai + pi * ar
    al_ref[0, 0:1, :] = pr
    al_ref[0, 1:2, :] = pi
    kft_ref[0] = _dot3(catr_ref[0], bbr, _NT) + _dot3(catn_ref[0], bbi, _NT)


def _s5_prep(lam_re, lam_im, log_dt, bmat_re, bmat_im, cmat_re, cmat_im):
    g, p, gc, L = N_GROUPS, P_STATE, GROUP_CH, SSM_CHUNK
    vec = pl.BlockSpec((1, 1, p), lambda i: (i, 0, 0))
    mat = pl.BlockSpec((1, gc, p), lambda i: (i, 0, 0))
    blk = lambda r, c: pl.BlockSpec((1, r, c), lambda i: (i, 0, 0))
    shp = lambda r, c: jax.ShapeDtypeStruct((g, r, c), F32)
    return pl.pallas_call(
        _s5_prep_kernel,
        grid=(g,),
        in_specs=[vec, vec, pl.BlockSpec((1, 1, 1), lambda i: (i, 0, 0)), mat, mat, mat, mat],
        out_specs=[blk(2, p), blk(2, p), blk(gc, p), blk(gc, p), blk(L * gc, p), blk(L * gc, p),
                   blk((L + 1) * gc, p), blk((L + 1) * gc, p), blk((L + 1) * gc, gc)],
        out_shape=[shp(2, p), shp(2, p), shp(gc, p), shp(gc, p), shp(L * gc, p), shp(L * gc, p),
                   shp((L + 1) * gc, p), shp((L + 1) * gc, p), shp((L + 1) * gc, gc)],
        compiler_params=_params("parallel"),
        name="s5_prep",
    )(lam_re.reshape(g, 1, p), lam_im.reshape(g, 1, p), log_dt.reshape(g, 1, 1),
      jnp.swapaxes(bmat_re, 1, 2), jnp.swapaxes(bmat_im, 1, 2), cmat_re, cmat_im)


def _s5_operators(prep):
    a, al, bbr, bbi, wsr, wsi, catr, catn, kft = prep
    g, p, gc, L = N_GROUPS, P_STATE, GROUP_CH, SSM_CHUNK
    no, go = g // GROUPS_PER_OCTET, GROUPS_PER_OCTET
    eye = jnp.eye(go, dtype=F32)
    eye_b = eye.astype(BF16)
    bd = lambda x, spec, rows: jnp.einsum(
        spec, x.astype(BF16).reshape(no, go, L, gc, -1), eye_b).reshape(no, rows, -1)
    k_lag = bd(kft[:, :L * gc], 'ogtcd,gh->otgdhc', L * go * gc)
    w_state = jnp.concatenate([bd(wsr, 'oglcp,gh->olgchp', L * go * gc),
                               bd(wsi, 'oglcp,gh->olgchp', L * go * gc)], axis=-1)
    w_carry = jnp.concatenate([bd(catr[:, gc:], 'oglcp,gh->ogplhc', go * p),
                               bd(catn[:, gc:], 'oglcp,gh->ogplhc', go * p)], axis=1)
    prompt_ops = (k_lag.astype(BF16), w_state.astype(BF16), w_carry.astype(BF16),
                  al[:, 0].reshape(1, g * p), al[:, 1].reshape(1, g * p))

    def in_bd(bt):
        return jnp.einsum('ogcp,gh->ogchp', bt.reshape(no, go, gc, p), eye).reshape(no, go * gc, go * p)

    def out_bd(ct):
        return jnp.einsum('ogcp,gh->ogphc', ct.reshape(no, go, gc, p), eye).reshape(no, go * p, go * gc)

    sample_ops = (a[:, 0].reshape(1, g * p), a[:, 1].reshape(1, g * p), in_bd(bbr), in_bd(bbi),
                  out_bd(catr[:, :gc]), out_bd(catn[:, :gc]))
    return prompt_ops, sample_ops


def _s5_scan_kernel(u_ref, klag_ref, wst_ref, wcar_ref, alr_ref, ali_ref, y_ref, hr_ref, hi_ref,
                    lagged, chunk_in, s_re, s_im, hp_re, hp_im, from_state, st_re, st_im, *, n_b, tt):
    L, W = SSM_CHUNK, LANES
    nc = tt // L
    hw = st_re.shape[-1]

    @pl.when(pl.program_id(1) == 0)
    def _():
        st_re[...] = jnp.zeros_like(st_re)
        st_im[...] = jnp.zeros_like(st_im)

    sub = lax.broadcasted_iota(jnp.int32, (nc, L, W), 1)
    for b in range(n_b):
        x3 = u_ref[b].reshape(nc, L, W)
        for lag in range(L):
            shifted = x3 if lag == 0 else jnp.where(sub >= lag, pltpu.roll(x3, lag, axis=1), 0.0)
            lagged[:, lag * W:(lag + 1) * W] = shifted.reshape(tt, W).astype(BF16)
        y_ref[b] = _dot(lagged[...], klag_ref[0])
        for l in range(L):
            chunk_in[l, pl.ds(b, nc, stride=n_b), :] = u_ref[b, pl.ds(l, nc, stride=L), :]

    s = _dot(jnp.concatenate([chunk_in[l].astype(BF16) for l in range(L)], axis=1), wst_ref[0])
    s_re[...] = s[:, :hw]
    s_im[...] = s[:, hw:]
    alr = jnp.broadcast_to(alr_ref[...], (n_b, hw))
    ali = jnp.broadcast_to(ali_ref[...], (n_b, hw))

    def step(k, carry):
        hr, hi = carry
        rows = pl.ds(pl.multiple_of(k * n_b, n_b), n_b)
        hp_re[rows, :] = hr
        hp_im[rows, :] = hi
        return (alr * hr - ali * hi + s_re[rows, :], alr * hi + ali * hr + s_im[rows, :])

    hr, hi = lax.fori_loop(0, nc, step, (st_re[...], st_im[...]))
    st_re[...] = hr
    st_im[...] = hi
    hr_ref[...] = hr
    hi_ref[...] = hi
    r = (_dot(hp_re[...].astype(BF16), wcar_ref[0, :hw, :])
         + _dot(hp_im[...].astype(BF16), wcar_ref[0, hw:, :]))
    for l in range(L):
        from_state[l] = r[:, l * W:(l + 1) * W]
    for b in range(n_b):
        for l in range(L):
            y_ref[b, pl.ds(l, nc, stride=L), :] += from_state[l, pl.ds(b, nc, stride=n_b), :]


def _s5_scan(u, ops):
    k_lag, w_state, w_carry, alr, ali = ops
    n_b, t, e = u.shape
    g, p, L, W = N_GROUPS, P_STATE, SSM_CHUNK, LANES
    no = e // W
    hw = GROUPS_PER_OCTET * p
    tt = min(S5_TIME_TILE, t)
    assert n_b == SUBLANES and L == SUBLANES and t % tt == 0 and GROUPS_PER_OCTET * GROUP_CH == W
    rows = n_b * tt // L
    per_octet = lambda r, c: pl.BlockSpec((1, r, c), lambda o, i: (o, 0, 0))
    state_vec = pl.BlockSpec((1, hw), lambda o, i: (0, o))
    state_out = pl.BlockSpec((n_b, hw), lambda o, i: (0, o))
    io = pl.BlockSpec((n_b, tt, W), lambda o, i: (0, i, o))
    y, hr, hi = pl.pallas_call(
        functools.partial(_s5_scan_kernel, n_b=n_b, tt=tt),
        grid=(no, t // tt),
        in_specs=[io, per_octet(L * W, W), per_octet(L * W, 2 * hw), per_octet(2 * hw, L * W),
                  state_vec, state_vec],
        out_specs=[io, state_out, state_out],
        out_shape=[jax.ShapeDtypeStruct((n_b, t, e), F32),
                   jax.ShapeDtypeStruct((n_b, g * p), F32),
                   jax.ShapeDtypeStruct((n_b, g * p), F32)],
        scratch_shapes=[pltpu.VMEM((tt, L * W), BF16), pltpu.VMEM((L, rows, W), F32),
                        pltpu.VMEM((rows, hw), F32), pltpu.VMEM((rows, hw), F32),
                        pltpu.VMEM((rows, hw), F32), pltpu.VMEM((rows, hw), F32),
                        pltpu.VMEM((L, rows, W), F32),
                        pltpu.VMEM((n_b, hw), F32), pltpu.VMEM((n_b, hw), F32)],
        compiler_params=_params("parallel", "arbitrary"),
        name="s5_scan",
    )(u, k_lag, w_state, w_carry, alr, ali)
    return y, hr.reshape(n_b, g, p), hi.reshape(n_b, g, p)


def _s5_step_kernel(u_ref, h0r_ref, h0i_ref, ar_ref, ai_ref, bbr_ref, bbi_ref, cr_ref, cn_ref,
                    y_ref, hr_ref, hi_ref):
    u = u_ref[...]
    ar, ai = ar_ref[...], ai_ref[...]
    h0r, h0i = h0r_ref[...], h0i_ref[...]
    hr = ar * h0r - ai * h0i + _dot3(u, bbr_ref[0])
    hi = ar * h0i + ai * h0r + _dot3(u, bbi_ref[0])
    hr_ref[...] = hr
    hi_ref[...] = hi
    y_ref[...] = _dot3(hr, cr_ref[0]) + _dot3(hi, cn_ref[0])


def _s5_step(u, h0_re, h0_im, ops):
    ar, ai, bbr, bbi, cr, cn = ops
    s, e = u.shape
    g, p = N_GROUPS, P_STATE
    no = g // GROUPS_PER_OCTET
    wc, ws = GROUPS_PER_OCTET * GROUP_CH, GROUPS_PER_OCTET * p
    assert wc == LANES
    ch = pl.BlockSpec((s, wc), lambda o: (0, o))
    st = pl.BlockSpec((s, ws), lambda o: (0, o))
    av = pl.BlockSpec((1, ws), lambda o: (0, o))
    w_in = pl.BlockSpec((1, wc, ws), lambda o: (o, 0, 0))
    w_out = pl.BlockSpec((1, ws, wc), lambda o: (o, 0, 0))
    y, hr, hi = pl.pallas_call(
        _s5_step_kernel,
        grid=(no,),
        in_specs=[ch, st, st, av, av, w_in, w_in, w_out, w_out],
        out_specs=[ch, st, st],
        out_shape=[jax.ShapeDtypeStruct((s, e), F32),
                   jax.ShapeDtypeStruct((s, g * p), F32),
                   jax.ShapeDtypeStruct((s, g * p), F32)],
        compiler_params=_params("parallel"),
        name="s5_step",
    )(u, h0_re.reshape(s, g * p), h0_im.reshape(s, g * p), ar, ai, bbr, bbi, cr, cn)
    return y, hr.reshape(s, g, p), hi.reshape(s, g, p)


def _sb_prompt_kernel(bias_ref, q_ref, k_ref, v_ref, u_ref, o_ref, acc, carry, w_s, z_s, *, tile, pairs):
    hg, qi = pl.program_id(1), pl.program_id(2)
    heads = range(2 * pairs)
    lanes = [slice((h // 2) * LANES, (h // 2 + 1) * LANES) for h in heads]
    first = lax.broadcasted_iota(jnp.int32, (tile, LANES), 1) < HEAD_DIM
    qm, bias = [], []
    for p in range(pairs):
        q2 = q_ref[0, :, p * LANES:(p + 1) * LANES]
        zero = jnp.zeros_like(q2)
        qm += [jnp.where(first, q2, zero), jnp.where(first, zero, q2)]
        bias += [bias_ref[2 * (hg * pairs + p)] * LOG2E, bias_ref[2 * (hg * pairs + p) + 1] * LOG2E]
    u = u_ref[...]
    row = lax.broadcasted_iota(jnp.int32, (tile, tile), 0)
    col = lax.broadcasted_iota(jnp.int32, (tile, tile), 1)
    below = col < row

    def block(j):
        return pl.ds(pl.multiple_of(j * tile, tile), tile)

    def weigh_values(j):
        for h in heads:
            acc[h] += _dot(w_s[h], v_ref[0, block(j), lanes[h]])

    def scores(j):
        return [lax.dot_general(qm[h], k_ref[0, block(j), lanes[h]], _NT, preferred_element_type=F32) + bias[h]
                for h in heads]

    def visit(j, diagonal):
        z = [z_s[h] for h in heads]
        z_next = scores(jnp.maximum(j - 1, 0))
        if not diagonal:
            weigh_values(j + 1)
        sp = [_softplus2(z[h]) for h in heads]
        if diagonal:
            sp = [jnp.where(below, sp[h], 0.0) for h in heads]
        cum = [_dot(sp[h].astype(BF16), u) for h in heads]
        total = [jnp.broadcast_to(cum[h][:, 0:1], (tile, LANES)) for h in heads]
        if diagonal:
            w = [jnp.where(below, jnp.exp2(z[h] + cum[h]), 0.0) for h in heads]
        else:
            w = [jnp.exp2(z[h] + (cum[h] + jnp.concatenate([carry[h]] * (tile // LANES), axis=1)))
                 for h in heads]
        for h in heads:
            w_s[h] = w[h].astype(BF16)
            z_s[h] = z_next[h]
            if diagonal:
                carry[h] = total[h]
            else:
                carry[h] += total[h]

    acc[...] = jnp.zeros_like(acc)
    for h, z0 in enumerate(scores(qi)):
        z_s[h] = z0
    visit(qi, True)

    def body(t, c):
        visit(qi - 1 - t, False)
        return c

    lax.fori_loop(0, qi, body, 0)
    weigh_values(0)
    for p in range(pairs):
        both = jnp.where(first, acc[2 * p], acc[2 * p + 1])
        o_ref[0, :, p * LANES:(p + 1) * LANES] = both.astype(o_ref.dtype)


def _sb_prompt(q, k, v, bias):
    n_b, t, e = q.shape
    tile = min(ATTN_TILE, t)
    pairs = ATTN_HEAD_PAIRS
    width = pairs * LANES
    assert 2 * HEAD_DIM == LANES and t % tile == 0 and tile % LANES == 0 and e % width == 0
    not_before = jnp.arange(tile)[:, None] >= jnp.arange(tile)[None, :]
    u = -not_before.astype(BF16)
    kv = pl.BlockSpec((1, t, width), lambda b, h, i: (b, 0, h))
    qo = pl.BlockSpec((1, tile, width), lambda b, h, i: (b, i, h))
    return pl.pallas_call(
        functools.partial(_sb_prompt_kernel, tile=tile, pairs=pairs),
        grid=(n_b, e // width, t // tile),
        in_specs=[pl.BlockSpec(memory_space=pltpu.SMEM), qo, kv, kv,
                  pl.BlockSpec((tile, tile), lambda b, h, i: (0, 0))],
        out_specs=qo,
        out_shape=jax.ShapeDtypeStruct((n_b, t, e), BF16),
        scratch_shapes=[pltpu.VMEM((2 * pairs, tile, LANES), F32),
                        pltpu.VMEM((2 * pairs, tile, LANES), F32),
                        pltpu.VMEM((2 * pairs, tile, tile), BF16),
                        pltpu.VMEM((2 * pairs, tile, tile), F32)],
        compiler_params=_params("parallel", "parallel", "parallel"),
        name="sb_prompt",
    )(bias, q, k, v, u)


def _sb_sample_kernel(pt_ref, q_ref, bias_ref, u_ref, *refs, n_pp):
    k_refs, v_refs = refs[:n_pp], refs[n_pp:2 * n_pp]
    o_ref, carry, acc = refs[2 * n_pp:]
    step = pl.program_id(1)

    @pl.when(step == 0)
    def _():
        carry[...] = jnp.zeros_like(carry)
        acc[...] = jnp.zeros_like(acc)

    q = q_ref[0]
    u = u_ref[...]
    pages = range(n_pp)
    z = [jnp.sum(k_refs[i][0] * q, axis=1) + bias_ref[...] for i in pages]
    sp = [_softplus(z[i]) for i in pages]
    parts = [_split_bf16(sp[i]) for i in pages]
    cum = [_dot(parts[i][0], u) + _dot(parts[i][1], u) for i in pages]
    run = carry[...]
    w = []
    for i in pages:
        w.append(jnp.exp(z[i] - sp[i] + (cum[i] + run)))
        run = run + jnp.broadcast_to(cum[i][:, 0:1] - sp[i][:, 0:1], run.shape)
    carry[...] = run
    part = acc[...]
    for i in pages:
        part = part + v_refs[i][0] * w[i][:, None, :]
    acc[...] = part

    @pl.when(step == pl.num_programs(1) - 1)
    def _():
        o_ref[0] = jnp.sum(acc[...], axis=2)


def _sb_sample(q, cache_k, cache_v, page_table, bias):
    s, e = q.shape
    n_phys, page, h, dh = cache_k.shape
    n_pages = page_table.shape[1]
    n_pp = PAGES_PER_STEP if n_pages % PAGES_PER_STEP == 0 else 1
    assert page == LANES
    k_t = cache_k.transpose(0, 2, 3, 1)
    v_t = cache_v.transpose(0, 2, 3, 1)
    q_lanes = jnp.broadcast_to(q.reshape(s, h, dh, 1), (s, h, dh, page))
    bias_lanes = jnp.broadcast_to(bias.reshape(h, 1), (h, page))
    t1 = jnp.arange(page)
    u = -(t1[:, None] > t1[None, :]).astype(BF16)

    def page_spec(i):
        def idx(b, g, pt):
            return (pt[b * n_pages + (n_pages - 1 - (g * n_pp + i))], 0, 0, 0)
        return pl.BlockSpec((1, h, dh, page), idx)

    pages = [page_spec(i) for i in range(n_pp)]
    const = lambda shape: pl.BlockSpec(shape, lambda b, g, pt: (0,) * len(shape))
    out = pl.pallas_call(
        functools.partial(_sb_sample_kernel, n_pp=n_pp),
        grid_spec=pltpu.PrefetchScalarGridSpec(
            num_scalar_prefetch=1,
            grid=(s, n_pages // n_pp),
            in_specs=[pl.BlockSpec((1, h, dh, page), lambda b, g, pt: (b, 0, 0, 0)),
                      const((h, page)), const((page, page))] + pages + pages,
            out_specs=pl.BlockSpec((1, h, dh), lambda b, g, pt: (b, 0, 0)),
            scratch_shapes=[pltpu.VMEM((h, page), F32), pltpu.VMEM((h, dh, page), F32)]),
        out_shape=jax.ShapeDtypeStruct((s, h, dh), F32),
        compiler_params=_params("parallel", "arbitrary"),
        name="sb_sample",
    )(page_table.reshape(-1), q_lanes, bias_lanes, u, *([k_t] * n_pp), *([v_t] * n_pp))
    return out.reshape(s, e)


def kernel(x_prompt, x_sample, c_prompt, c_sample, state_ssm_re, state_ssm_im, cache_k, cache_v, page_table, g_pre_a, g_post_a, w_mod_a, b_mod_a, w_in_a, lam_re, lam_im, log_dt, bmat_re, bmat_im, cmat_re, cmat_im, d_skip, w_glu, b_glu, w_out_a, g_kv, w_mod_kv, b_mod_kv, w_kv, g_pre_b, g_post_b, w_mod_b, b_mod_b, w_in_b, sb_bias, w_out_b):
    n_b, t, d = x_prompt.shape
    n_s = x_sample.shape[0]
    n_a, n_bl = w_in_a.shape[0], w_in_b.shape[0]
    e = d
    q_scale = HEAD_DIM ** -0.5
    bf = lambda w: w.astype(BF16)

    c_all = jnp.concatenate([c_prompt, c_sample], axis=0)

    def mods(w_mod, b_mod):
        m = _ada_mod(c_all, w_mod, b_mod)
        return m[:n_b].reshape(n_b, 1, -1), m[n_b:].reshape(1, n_s, -1)

    xp = x_prompt
    xs = x_sample.reshape(1, n_s, d)
    p_re, p_im, s_re, s_im = [], [], [], []
    uz_cfg = ((0, e, ((0, 1.0),)), (e, e, ((1, 1.0),)))
    for i in range(n_a):
        mp, ms = mods(w_mod_a[i], b_mod_a[i])
        w_in, w_g, w_o = bf(w_in_a[i]), bf(w_glu[i]), bf(w_out_a[i])
        prompt_ops, sample_ops = _s5_operators(
            _s5_prep(lam_re[i], lam_im[i], log_dt[i], bmat_re[i], bmat_im[i], cmat_re[i], cmat_im[i]))
        u, z = _norm_mm(xp, mp, g_pre_a[i], w_in, uz_cfg, (F32, BF16))
        y, hr, hi = _s5_scan(u, prompt_ops)
        xp = _s5_post(y, u, z, d_skip[i], w_g, b_glu[i], w_o, g_post_a[i], mp, xp)
        p_re.append(hr)
        p_im.append(hi)
        u, z = _norm_mm(xs, ms, g_pre_a[i], w_in, uz_cfg, (F32, F32))
        y, hr, hi = _s5_step(u[0], state_ssm_re[i], state_ssm_im[i], sample_ops)
        xs = _s5_post(y.reshape(1, n_s, e), u, z, d_skip[i], w_g, b_glu[i], w_o, g_post_a[i], ms, xs)
        s_re.append(hr)
        s_im.append(hi)

    mp, ms = mods(w_mod_kv, b_mod_kv)
    w = bf(w_kv)
    kv_cfg = ((0, e, ((0, 1.0), (2, 1.0))), (e, e, ((1, 1.0), (3, 1.0))))
    p_k, p_v, pk_bf, pv_bf = _norm_mm(xp, mp, g_kv, w, kv_cfg, (F32, F32, BF16, BF16))
    kv_cfg_s = ((0, e, ((0, 1.0),)), (e, e, ((1, 1.0),)))
    s_k, s_v = _norm_mm(xs, ms, g_kv, w, kv_cfg_s, (F32, F32))

    qz_cfg = ((0, e, ((0, q_scale),)), (e, e, ((1, 1.0),)))
    qz_cfg_p = ((0, e, ((0, q_scale * LOG2E),)), (e, e, ((1, 1.0),)))
    for j in range(n_bl):
        mp, ms = mods(w_mod_b[j], b_mod_b[j])
        w_in, w_o = bf(w_in_b[j]), bf(w_out_b[j])
        q, z = _norm_mm(xp, mp, g_pre_b[j], w_in, qz_cfg_p, (BF16, BF16))
        a = _sb_prompt(q, pk_bf, pv_bf, sb_bias[j])
        xp = _sb_post(a, z, w_o, g_post_b[j], mp, xp)
        q, z = _norm_mm(xs, ms, g_pre_b[j], w_in, qz_cfg, (F32, F32))
        a = _sb_sample(q[0], cache_k, cache_v, page_table, sb_bias[j])
        xs = _sb_post(a.reshape(1, n_s, e), z, w_o, g_post_b[j], ms, xs)

    return (xp, xs.reshape(n_s, 1, d), jnp.stack(p_re), jnp.stack(p_im),
            p_k.reshape(n_b, t, N_HEADS, HEAD_DIM), p_v.reshape(n_b, t, N_HEADS, HEAD_DIM),
            jnp.stack(s_re), jnp.stack(s_im),
            s_k.reshape(n_s, 1, N_HEADS, HEAD_DIM), s_v.reshape(n_s, 1, N_HEADS, HEAD_DIM))
```

```python
import functools

import jax
import jax.numpy as jnp
from jax import lax
from jax.experimental import pallas as pl
from jax.experimental.pallas import tpu as pltpu

F32 = jnp.float32
BF16 = jnp.bfloat16

D_MODEL = 1024
GROUP_CH = 16
N_GROUPS = D_MODEL // GROUP_CH
P_STATE = 64
N_HEADS = 16
HEAD_DIM = D_MODEL // N_HEADS
EPS = 1e-6

SSM_CHUNK = 8
S5_TIME_TILE = 512
GROUPS_PER_OCTET = 8
LANES = 128
SUBLANES = 8
ROW_TILE = 512
ATTN_TILE = 256
ATTN_HEAD_PAIRS = 4
LOG2E = 1.4426950408889634
PAGES_PER_STEP = 8
VMEM_LIMIT = 48 * 1024 * 1024

_NT = (((1,), (1,)), ((), ()))


def _params(*sem):
    return pltpu.CompilerParams(dimension_semantics=sem, vmem_limit_bytes=VMEM_LIMIT)


def _sigmoid(x):
    return 0.5 * jnp.tanh(0.5 * x) + 0.5


def _softplus(z):
    return jnp.maximum(z, 0.0) + jnp.log(1.0 + jnp.exp(-jnp.abs(z)))


def _softplus2(z):
    return jnp.maximum(z, 0.0) + jnp.log2(1.0 + jnp.exp2(-jnp.abs(z)))


_GELU_C1 = 0.7978845608028654
_GELU_C3 = _GELU_C1 * 0.044715


def _gelu_tanh(x):
    half = 0.5 * x
    return half + half * jnp.tanh(x * (_GELU_C1 + _GELU_C3 * (x * x)))


def _split_bf16(x):
    hi = x.astype(BF16)
    lo = (x - hi.astype(F32)).astype(BF16)
    return hi, lo


def _dot(a, b):
    return jnp.dot(a, b, preferred_element_type=F32)


def _dot3(a, b, dims=None):
    a_hi, a_lo = _split_bf16(a)
    b_hi, b_lo = _split_bf16(b)
    if dims is None:
        f = _dot
    else:
        f = lambda p, q: lax.dot_general(p, q, dims, preferred_element_type=F32)
    return f(a_hi, b_hi) + (f(a_hi, b_lo) + f(a_lo, b_hi))


def _rms(x, g):
    return x * lax.rsqrt(jnp.mean(x * x, axis=-1, keepdims=True) + EPS) * g


def _mod_kernel(c_ref, w_ref, b_ref, o_ref):
    c = c_ref[...]
    s = (c * _sigmoid(c)).astype(BF16)
    o_ref[...] = _dot(s, w_ref[0].astype(BF16)) + b_ref[...]


def _ada_mod(c, w_layers, layer, b):
    m, d = c.shape
    n = w_layers.shape[2]
    tn = 1024
    return pl.pallas_call(
        _mod_kernel,
        grid=(n // tn,),
        in_specs=[pl.BlockSpec((m, d), lambda j: (0, 0)),
                  pl.BlockSpec((1, d, tn), lambda j: (layer, 0, j)),
                  pl.BlockSpec((1, tn), lambda j: (0, j))],
        out_specs=pl.BlockSpec((m, tn), lambda j: (0, j)),
        out_shape=jax.ShapeDtypeStruct((m, n), F32),
        compiler_params=_params("parallel"),
        name="ada_mod",
    )(c, w_layers, b.reshape(1, n))


def _norm_mm_kernel(x_ref, sh_ref, sc_ref, g_ref, w_ref, *out_refs, cfg, nchunk):
    h = _rms(x_ref[0], g_ref[...]) * (1.0 + sc_ref[0]) + sh_ref[0]
    hb = h.astype(BF16)
    for col0, ncols, outs in cfg:
        for j in range(0, ncols, nchunk):
            r = _dot(hb, w_ref[:, col0 + j:col0 + j + nchunk])
            for oi, scale in outs:
                v = r if scale == 1.0 else r * scale
                out_refs[oi][0, :, j:j + nchunk] = v.astype(out_refs[oi].dtype)


def _norm_mm(x, m, g, w, cfg, out_dtypes):
    bx, t, d = x.shape
    tm = min(ROW_TILE, t)
    per_row = m.shape[1] != 1
    tmm = tm if per_row else 1
    mod_idx = (lambda col: (lambda b, i: (b, i, col))) if per_row else (lambda col: (lambda b, i: (b, 0, col)))
    widths = {}
    for _, ncols, outs in cfg:
        for oi, _ in outs:
            widths[oi] = ncols
    n_out = len(out_dtypes)
    return pl.pallas_call(
        functools.partial(_norm_mm_kernel, cfg=cfg, nchunk=512),
        grid=(bx, t // tm),
        in_specs=[pl.BlockSpec((1, tm, d), lambda b, i: (b, i, 0)),
                  pl.BlockSpec((1, tmm, d), mod_idx(0)),
                  pl.BlockSpec((1, tmm, d), mod_idx(1)),
                  pl.BlockSpec((1, d), lambda b, i: (0, 0)),
                  pl.BlockSpec(w.shape, lambda b, i: (0, 0))],
        out_specs=[pl.BlockSpec((1, tm, widths[oi]), lambda b, i: (b, i, 0)) for oi in range(n_out)],
        out_shape=[jax.ShapeDtypeStruct((bx, t, widths[oi]), out_dtypes[oi]) for oi in range(n_out)],
        compiler_params=_params("parallel", "parallel"),
        name="norm_proj",
    )(x, m, m, g.reshape(1, d), w)


def _residual_tail(y, w_out_ref, g_ref, gate_ref, x_ref, o_ref):
    o = _dot(y.astype(BF16), w_out_ref[...])
    o_ref[0] = x_ref[0] + gate_ref[0] * _rms(o, g_ref[...])


def _s5_post_kernel(y_ref, u_ref, z_ref, dsk_ref, wglu_ref, bglu_ref, wout_ref, g_ref, gate_ref,
                    x_ref, o_ref):
    y = _gelu_tanh(y_ref[0] + dsk_ref[...] * u_ref[0])
    y = y * _sigmoid(_dot(y.astype(BF16), wglu_ref[...]) + bglu_ref[...])
    z = z_ref[0].astype(F32)
    y = y * (z * _sigmoid(z))
    _residual_tail(y, wout_ref, g_ref, gate_ref, x_ref, o_ref)


def _sb_post_kernel(a_ref, z_ref, wout_ref, g_ref, gate_ref, x_ref, o_ref):
    z = z_ref[0].astype(F32)
    y = a_ref[0].astype(F32) * (z * _sigmoid(z))
    _residual_tail(y, wout_ref, g_ref, gate_ref, x_ref, o_ref)


def _row_specs(x, m):
    bx, t, d = x.shape
    tm = min(ROW_TILE, t)
    per_row = m.shape[1] != 1
    tmm = tm if per_row else 1
    gate_idx = (lambda b, i: (b, i, 2)) if per_row else (lambda b, i: (b, 0, 2))
    row = lambda col: pl.BlockSpec((1, tm, d), lambda b, i: (b, i, col))
    vec = pl.BlockSpec((1, d), lambda b, i: (0, 0))
    mat = pl.BlockSpec((d, d), lambda b, i: (0, 0))
    gate = pl.BlockSpec((1, tmm, d), gate_idx)
    return (bx, t // tm), row, vec, mat, gate


def _s5_post(y, u, z, d_skip, w_glu, b_glu, w_out, g_post, m, x):
    grid, row, vec, mat, gate = _row_specs(x, m)
    d = x.shape[-1]
    return pl.pallas_call(
        _s5_post_kernel,
        grid=grid,
        in_specs=[row(0), row(0), row(0), vec, mat, vec, mat, vec, gate, row(0)],
        out_specs=row(0),
        out_shape=jax.ShapeDtypeStruct(x.shape, F32),
        compiler_params=_params("parallel", "parallel"),
        name="s5_post",
    )(y, u, z, d_skip.reshape(1, d), w_glu, b_glu.reshape(1, d), w_out, g_post.reshape(1, d), m, x)


def _sb_post(a, z, w_out, g_post, m, x):
    grid, row, vec, mat, gate = _row_specs(x, m)
    d = x.shape[-1]
    return pl.pallas_call(
        _sb_post_kernel,
        grid=grid,
        in_specs=[row(0), row(0), mat, vec, gate, row(0)],
        out_specs=row(0),
        out_shape=jax.ShapeDtypeStruct(x.shape, F32),
        compiler_params=_params("parallel", "parallel"),
        name="sb_post",
    )(a, z, w_out, g_post.reshape(1, d), m, x)


def _s5_prep_kernel(lre_ref, lim_ref, ldt_ref, btr_ref, bti_ref, cre_ref, cim_ref,
                    a_ref, al_ref, bbr_ref, bbi_ref, wsr_ref, wsi_ref, catr_ref, catn_ref, kft_ref):
    L = SSM_CHUNK
    lr, li = lre_ref[0], lim_ref[0]
    dt = jnp.exp(ldt_ref[0])
    mag = jnp.exp(lr * dt)
    ar, ai = mag * jnp.cos(li * dt), mag * jnp.sin(li * dt)
    den = lr * lr + li * li
    cr = ((ar - 1.0) * lr + ai * li) / den
    ci = (ai * lr - (ar - 1.0) * li) / den
    btr, bti = btr_ref[0], bti_ref[0]
    bbr, bbi = cr * btr - ci * bti, cr * bti + ci * btr
    cre, cim = cre_ref[0], cim_ref[0]
    a_ref[0, 0:1, :] = ar
    a_ref[0, 1:2, :] = ai
    bbr_ref[0] = bbr
    bbi_ref[0] = bbi
    pr, pi = jnp.ones_like(ar), jnp.zeros_like(ar)
    for t in range(L + 1):
        rows = slice(t * GROUP_CH, (t + 1) * GROUP_CH)
        catr_ref[0, rows, :] = cre * pr - cim * pi
        catn_ref[0, rows, :] = -(cre * pi + cim * pr)
        if t < L:
            wrows = slice((L - 1 - t) * GROUP_CH, (L - t) * GROUP_CH)
            wsr_ref[0, wrows, :] = pr * bbr - pi * bbi
            wsi_ref[0, wrows, :] = pr * bbi + pi * bbr
            pr, pi = pr * ar - pi * ai, pr * ai + pi * ar
    al_ref[0, 0:1, :] = pr
    al_ref[0, 1:2, :] = pi
    kft_ref[0] = _dot3(catr_ref[0], bbr, _NT) + _dot3(catn_ref[0], bbi, _NT)


def _s5_prep(lam_re, lam_im, log_dt, bmat_re, bmat_im, cmat_re, cmat_im):
    g, p, gc, L = N_GROUPS, P_STATE, GROUP_CH, SSM_CHUNK
    vec = pl.BlockSpec((1, 1, p), lambda i: (i, 0, 0))
    mat = pl.BlockSpec((1, gc, p), lambda i: (i, 0, 0))
    blk = lambda r, c: pl.BlockSpec((1, r, c), lambda i: (i, 0, 0))
    shp = lambda r, c: jax.ShapeDtypeStruct((g, r, c), F32)
    return pl.pallas_call(
        _s5_prep_kernel,
        grid=(g,),
        in_specs=[vec, vec, pl.BlockSpec((1, 1, 1), lambda i: (i, 0, 0)), mat, mat, mat, mat],
        out_specs=[blk(2, p), blk(2, p), blk(gc, p), blk(gc, p), blk(L * gc, p), blk(L * gc, p),
                   blk((L + 1) * gc, p), blk((L + 1) * gc, p), blk((L + 1) * gc, gc)],
        out_shape=[shp(2, p), shp(2, p), shp(gc, p), shp(gc, p), shp(L * gc, p), shp(L * gc, p),
                   shp((L + 1) * gc, p), shp((L + 1) * gc, p), shp((L + 1) * gc, gc)],
        compiler_params=_params("parallel"),
        name="s5_prep",
    )(lam_re.reshape(g, 1, p), lam_im.reshape(g, 1, p), log_dt.reshape(g, 1, 1),
      jnp.swapaxes(bmat_re, 1, 2), jnp.swapaxes(bmat_im, 1, 2), cmat_re, cmat_im)


def _s5_operators(prep):
    a, al, bbr, bbi, wsr, wsi, catr, catn, kft = prep
    g, p, gc, L = N_GROUPS, P_STATE, GROUP_CH, SSM_CHUNK
    no, go = g // GROUPS_PER_OCTET, GROUPS_PER_OCTET
    eye = jnp.eye(go, dtype=F32)

    def by_step(x):
        return x.reshape(no, go, L, gc, -1).transpose(0, 2, 1, 3, 4).reshape(no, L * go * gc, -1)

    def by_state(x):
        return x[:, gc:].reshape(no, go, L * gc, p).transpose(0, 1, 3, 2).reshape(no, go * p, L * gc)

    k_lag = kft[:, :L * gc].reshape(no, go, L, gc, gc).transpose(0, 2, 1, 4, 3).reshape(no, L * go * gc, gc)
    w_state = jnp.concatenate([by_step(wsr), by_step(wsi)], axis=-1)
    w_carry = jnp.concatenate([by_state(catr), by_state(catn)], axis=1)
    prompt_ops = (k_lag.astype(BF16), w_state.astype(BF16), w_carry.astype(BF16),
                  al[:, 0].reshape(1, g * p), al[:, 1].reshape(1, g * p))

    def in_bd(bt):
        return jnp.einsum('ogcp,gh->ogchp', bt.reshape(no, go, gc, p), eye).reshape(no, go * gc, go * p)

    def out_bd(ct):
        return jnp.einsum('ogcp,gh->ogphc', ct.reshape(no, go, gc, p), eye).reshape(no, go * p, go * gc)

    sample_ops = (a[:, 0].reshape(1, g * p), a[:, 1].reshape(1, g * p), in_bd(bbr), in_bd(bbi),
                  out_bd(catr[:, :gc]), out_bd(catn[:, :gc]))
    return prompt_ops, sample_ops


def _spread_blocks(compact, row_group, col_group, col_source):
    n_rows, n_src = compact.shape
    n_cols = col_group[0]
    src = lax.broadcasted_iota(jnp.int32, (n_src, n_cols), 0)
    col = lax.broadcasted_iota(jnp.int32, (n_src, n_cols), 1)
    copy = jnp.where(src == col_source(col), 1.0, 0.0).astype(BF16)
    r = lax.broadcasted_iota(jnp.int32, (n_rows, n_cols), 0)
    j = lax.broadcasted_iota(jnp.int32, (n_rows, n_cols), 1)
    same = row_group(r) == col_group[1](j)
    return jnp.where(same, _dot(compact, copy), 0.0).astype(BF16)


def _s5_scan_kernel(u_ref, klag_c, wst_c, wcar_c, alr_ref, ali_ref, y_ref, hr_ref, hi_ref,
                    lagged, chunk_in, s_re, s_im, hp_re, hp_im, from_state, st_re, st_im,
                    klag_ref, wst_ref, wcar_ref, *, n_b, tt):
    L, W = SSM_CHUNK, LANES
    nc = tt // L
    hw = st_re.shape[-1]

    @pl.when(pl.program_id(1) == 0)
    def _():
        st_re[...] = jnp.zeros_like(st_re)
        st_im[...] = jnp.zeros_like(st_im)
        by_channel = lambda r: (r >> 4) & 7
        klag_ref[...] = _spread_blocks(klag_c[0], by_channel, (W, lambda j: j >> 4), lambda j: j & 15)
        wst_ref[...] = _spread_blocks(wst_c[0], by_channel, (2 * hw, lambda j: (j >> 6) & 7),
                                      lambda j: ((j >> 9) << 6) | (j & 63))
        wcar_ref[...] = _spread_blocks(wcar_c[0], lambda r: (r >> 6) & 7, (L * W, lambda j: (j >> 4) & 7),
                                       lambda j: ((j >> 7) << 4) | (j & 15))

    sub = lax.broadcasted_iota(jnp.int32, (nc, L, W), 1)
    for b in range(n_b):
        x3 = u_ref[b].reshape(nc, L, W)
        for lag in range(L):
            shifted = x3 if lag == 0 else jnp.where(sub >= lag, pltpu.roll(x3, lag, axis=1), 0.0)
            lagged[:, lag * W:(lag + 1) * W] = shifted.reshape(tt, W).astype(BF16)
        y_ref[b] = _dot(lagged[...], klag_ref[...])
        for l in range(L):
            chunk_in[l, pl.ds(b, nc, stride=n_b), :] = u_ref[b, pl.ds(l, nc, stride=L), :]

    s = _dot(jnp.concatenate([chunk_in[l].astype(BF16) for l in range(L)], axis=1), wst_ref[...])
    s_re[...] = s[:, :hw]
    s_im[...] = s[:, hw:]
    alr = jnp.broadcast_to(alr_ref[...], (n_b, hw))
    ali = jnp.broadcast_to(ali_ref[...], (n_b, hw))

    def step(k, carry):
        hr, hi = carry
        rows = pl.ds(pl.multiple_of(k * n_b, n_b), n_b)
        hp_re[rows, :] = hr
        hp_im[rows, :] = hi
        return (alr * hr - ali * hi + s_re[rows, :], alr * hi + ali * hr + s_im[rows, :])

    hr, hi = lax.fori_loop(0, nc, step, (st_re[...], st_im[...]))
    st_re[...] = hr
    st_im[...] = hi
    hr_ref[...] = hr
    hi_ref[...] = hi
    r = (_dot(hp_re[...].astype(BF16), wcar_ref[:hw, :])
         + _dot(hp_im[...].astype(BF16), wcar_ref[hw:, :]))
    for l in range(L):
        from_state[l] = r[:, l * W:(l + 1) * W]
    for b in range(n_b):
        for l in range(L):
            y_ref[b, pl.ds(l, nc, stride=L), :] += from_state[l, pl.ds(b, nc, stride=n_b), :]


def _s5_scan(u, ops):
    k_lag, w_state, w_carry, alr, ali = ops
    n_b, t, e = u.shape
    g, p, L, W = N_GROUPS, P_STATE, SSM_CHUNK, LANES
    no = e // W
    hw = GROUPS_PER_OCTET * p
    tt = min(S5_TIME_TILE, t)
    assert n_b == SUBLANES and L == SUBLANES and t % tt == 0 and GROUPS_PER_OCTET * GROUP_CH == W
    rows = n_b * tt // L
    per_octet = lambda r, c: pl.BlockSpec((1, r, c), lambda o, i: (o, 0, 0))
    state_vec = pl.BlockSpec((1, hw), lambda o, i: (0, o))
    state_out = pl.BlockSpec((n_b, hw), lambda o, i: (0, o))
    io = pl.BlockSpec((n_b, tt, W), lambda o, i: (0, i, o))
    y, hr, hi = pl.pallas_call(
        functools.partial(_s5_scan_kernel, n_b=n_b, tt=tt),
        grid=(no, t // tt),
        in_specs=[io, per_octet(L * W, GROUP_CH), per_octet(L * W, 2 * p), per_octet(2 * hw, L * GROUP_CH),
                  state_vec, state_vec],
        out_specs=[io, state_out, state_out],
        out_shape=[jax.ShapeDtypeStruct((n_b, t, e), F32),
                   jax.ShapeDtypeStruct((n_b, g * p), F32),
                   jax.ShapeDtypeStruct((n_b, g * p), F32)],
        scratch_shapes=[pltpu.VMEM((tt, L * W), BF16), pltpu.VMEM((L, rows, W), F32),
                        pltpu.VMEM((rows, hw), F32), pltpu.VMEM((rows, hw), F32),
                        pltpu.VMEM((rows, hw), F32), pltpu.VMEM((rows, hw), F32),
                        pltpu.VMEM((L, rows, W), F32),
                        pltpu.VMEM((n_b, hw), F32), pltpu.VMEM((n_b, hw), F32),
                        pltpu.VMEM((L * W, W), BF16), pltpu.VMEM((L * W, 2 * hw), BF16),
                        pltpu.VMEM((2 * hw, L * W), BF16)],
        compiler_params=_params("parallel", "arbitrary"),
        name="s5_scan",
    )(u, k_lag, w_state, w_carry, alr, ali)
    return y, hr.reshape(n_b, g, p), hi.reshape(n_b, g, p)


def _s5_step_kernel(u_ref, h0r_ref, h0i_ref, ar_ref, ai_ref, bbr_ref, bbi_ref, cr_ref, cn_ref,
                    y_ref, hr_ref, hi_ref):
    u = u_ref[...]
    ar, ai = ar_ref[...], ai_ref[...]
    h0r, h0i = h0r_ref[...], h0i_ref[...]
    hr = ar * h0r - ai * h0i + _dot3(u, bbr_ref[0])
    hi = ar * h0i + ai * h0r + _dot3(u, bbi_ref[0])
    hr_ref[...] = hr
    hi_ref[...] = hi
    y_ref[...] = _dot3(hr, cr_ref[0]) + _dot3(hi, cn_ref[0])


def _s5_step(u, h0_re, h0_im, ops):
    ar, ai, bbr, bbi, cr, cn = ops
    s, e = u.shape
    g, p = N_GROUPS, P_STATE
    no = g // GROUPS_PER_OCTET
    wc, ws = GROUPS_PER_OCTET * GROUP_CH, GROUPS_PER_OCTET * p
    assert wc == LANES
    ch = pl.BlockSpec((s, wc), lambda o: (0, o))
    st = pl.BlockSpec((s, ws), lambda o: (0, o))
    av = pl.BlockSpec((1, ws), lambda o: (0, o))
    w_in = pl.BlockSpec((1, wc, ws), lambda o: (o, 0, 0))
    w_out = pl.BlockSpec((1, ws, wc), lambda o: (o, 0, 0))
    y, hr, hi = pl.pallas_call(
        _s5_step_kernel,
        grid=(no,),
        in_specs=[ch, st, st, av, av, w_in, w_in, w_out, w_out],
        out_specs=[ch, st, st],
        out_shape=[jax.ShapeDtypeStruct((s, e), F32),
                   jax.ShapeDtypeStruct((s, g * p), F32),
                   jax.ShapeDtypeStruct((s, g * p), F32)],
        compiler_params=_params("parallel"),
        name="s5_step",
    )(u, h0_re.reshape(s, g * p), h0_im.reshape(s, g * p), ar, ai, bbr, bbi, cr, cn)
    return y, hr.reshape(s, g, p), hi.reshape(s, g, p)


def _sb_prompt_kernel(bias_ref, q_ref, k_ref, v_ref, u_ref, o_ref, acc, carry, w_s, z_s, *, tile, pairs):
    hg, qi = pl.program_id(1), pl.program_id(2)
    heads = range(2 * pairs)
    lanes = [slice((h // 2) * LANES, (h // 2 + 1) * LANES) for h in heads]
    first = lax.broadcasted_iota(jnp.int32, (tile, LANES), 1) < HEAD_DIM
    qm, bias = [], []
    for p in range(pairs):
        q2 = q_ref[0, :, p * LANES:(p + 1) * LANES]
        zero = jnp.zeros_like(q2)
        qm += [jnp.where(first, q2, zero), jnp.where(first, zero, q2)]
        bias += [bias_ref[2 * (hg * pairs + p)] * LOG2E, bias_ref[2 * (hg * pairs + p) + 1] * LOG2E]
    u = u_ref[...]
    row = lax.broadcasted_iota(jnp.int32, (tile, tile), 0)
    col = lax.broadcasted_iota(jnp.int32, (tile, tile), 1)
    below = col < row

    def block(j):
        return pl.ds(pl.multiple_of(j * tile, tile), tile)

    def weigh_values(j):
        for h in heads:
            acc[h] += _dot(w_s[h], v_ref[0, block(j), lanes[h]])

    def scores(j):
        return [lax.dot_general(qm[h], k_ref[0, block(j), lanes[h]], _NT, preferred_element_type=F32) + bias[h]
                for h in heads]

    def visit(j, diagonal):
        z = [z_s[h] for h in heads]
        z_next = scores(jnp.maximum(j - 1, 0))
        if not diagonal:
            weigh_values(j + 1)
        sp = [_softplus2(z[h]) for h in heads]
        if diagonal:
            sp = [jnp.where(below, sp[h], 0.0) for h in heads]
        cum = [_dot(sp[h].astype(BF16), u) for h in heads]
        total = [jnp.broadcast_to(cum[h][:, 0:1], (tile, LANES)) for h in heads]
        if diagonal:
            w = [jnp.where(below, jnp.exp2(z[h] + cum[h]), 0.0) for h in heads]
        else:
            w = [jnp.exp2(z[h] + (cum[h] + jnp.concatenate([carry[h]] * (tile // LANES), axis=1)))
                 for h in heads]
        for h in heads:
            w_s[h] = w[h].astype(BF16)
            z_s[h] = z_next[h]
            if diagonal:
                carry[h] = total[h]
            else:
                carry[h] += total[h]

    acc[...] = jnp.zeros_like(acc)
    for h, z0 in enumerate(scores(qi)):
        z_s[h] = z0
    visit(qi, True)

    def body(t, c):
        visit(qi - 1 - t, False)
        return c

    lax.fori_loop(0, qi, body, 0)
    weigh_values(0)
    for p in range(pairs):
        both = jnp.where(first, acc[2 * p], acc[2 * p + 1])
        o_ref[0, :, p * LANES:(p + 1) * LANES] = both.astype(o_ref.dtype)


def _sb_prompt(q, k, v, bias):
    n_b, t, e = q.shape
    tile = min(ATTN_TILE, t)
    pairs = ATTN_HEAD_PAIRS
    width = pairs * LANES
    assert 2 * HEAD_DIM == LANES and t % tile == 0 and tile % LANES == 0 and e % width == 0
    not_before = jnp.arange(tile)[:, None] >= jnp.arange(tile)[None, :]
    u = -not_before.astype(BF16)
    kv = pl.BlockSpec((1, t, width), lambda b, h, i: (b, 0, h))
    qo = pl.BlockSpec((1, tile, width), lambda b, h, i: (b, i, h))
    return pl.pallas_call(
        functools.partial(_sb_prompt_kernel, tile=tile, pairs=pairs),
        grid=(n_b, e // width, t // tile),
        in_specs=[pl.BlockSpec(memory_space=pltpu.SMEM), qo, kv, kv,
                  pl.BlockSpec((tile, tile), lambda b, h, i: (0, 0))],
        out_specs=qo,
        out_shape=jax.ShapeDtypeStruct((n_b, t, e), BF16),
        scratch_shapes=[pltpu.VMEM((2 * pairs, tile, LANES), F32),
                        pltpu.VMEM((2 * pairs, tile, LANES), F32),
                        pltpu.VMEM((2 * pairs, tile, tile), BF16),
                        pltpu.VMEM((2 * pairs, tile, tile), F32)],
        compiler_params=_params("parallel", "parallel", "parallel"),
        name="sb_prompt",
    )(bias, q, k, v, u)


def _sb_sample_kernel(pt_ref, q_ref, bias_ref, u_ref, *refs, n_pp):
    k_refs, v_refs = refs[:n_pp], refs[n_pp:2 * n_pp]
    o_ref, carry, acc, q_lanes = refs[2 * n_pp:]
    step = pl.program_id(1)
    n_h, dh, page = q_lanes.shape

    @pl.when(step == 0)
    def _():
        carry[...] = jnp.zeros_like(carry)
        acc[...] = jnp.zeros_like(acc)
        on_diag = (lax.broadcasted_iota(jnp.int32, (dh, dh), 0) == lax.broadcasted_iota(jnp.int32, (dh, dh), 1))
        ones = jnp.ones((dh, page), BF16)
        for hh in range(n_h):
            hi, lo = _split_bf16(jnp.where(on_diag, jnp.broadcast_to(q_ref[0, hh:hh + 1, :], (dh, dh)), 0.0))
            q_lanes[hh] = _dot(hi, ones) + _dot(lo, ones)

    q = q_lanes[...]
    u = u_ref[...]
    pages = range(n_pp)
    z = [jnp.sum(k_refs[i][0] * q, axis=1) + bias_ref[...] for i in pages]
    sp = [_softplus(z[i]) for i in pages]
    parts = [_split_bf16(sp[i]) for i in pages]
    cum = [_dot(parts[i][0], u) + _dot(parts[i][1], u) for i in pages]
    run = carry[...]
    w = []
    for i in pages:
        w.append(jnp.exp(z[i] - sp[i] + (cum[i] + run)))
        run = run + jnp.broadcast_to(cum[i][:, 0:1] - sp[i][:, 0:1], run.shape)
    carry[...] = run
    part = acc[...]
    for i in pages:
        part = part + v_refs[i][0] * w[i][:, None, :]
    acc[...] = part

    @pl.when(step == pl.num_programs(1) - 1)
    def _():
        o_ref[0] = jnp.sum(acc[...], axis=2)


def _sb_sample(q, cache_k, cache_v, page_table, bias):
    s, e = q.shape
    n_phys, page, h, dh = cache_k.shape
    n_pages = page_table.shape[1]
    n_pp = PAGES_PER_STEP if n_pages % PAGES_PER_STEP == 0 else 1
    assert page == LANES
    k_t = cache_k.transpose(0, 2, 3, 1)
    v_t = cache_v.transpose(0, 2, 3, 1)
    bias_lanes = jnp.broadcast_to(bias.reshape(h, 1), (h, page))
    t1 = jnp.arange(page)
    u = -(t1[:, None] > t1[None, :]).astype(BF16)

    def page_spec(i):
        def idx(b, g, pt):
            return (pt[b * n_pages + (n_pages - 1 - (g * n_pp + i))], 0, 0, 0)
        return pl.BlockSpec((1, h, dh, page), idx)

    pages = [page_spec(i) for i in range(n_pp)]
    const = lambda shape: pl.BlockSpec(shape, lambda b, g, pt: (0,) * len(shape))
    out = pl.pallas_call(
        functools.partial(_sb_sample_kernel, n_pp=n_pp),
        grid_spec=pltpu.PrefetchScalarGridSpec(
            num_scalar_prefetch=1,
            grid=(s, n_pages // n_pp),
            in_specs=[pl.BlockSpec((1, h, dh), lambda b, g, pt: (b, 0, 0)),
                      const((h, page)), const((page, page))] + pages + pages,
            out_specs=pl.BlockSpec((1, h, dh), lambda b, g, pt: (b, 0, 0)),
            scratch_shapes=[pltpu.VMEM((h, page), F32), pltpu.VMEM((h, dh, page), F32),
                            pltpu.VMEM((h, dh, page), F32)]),
        out_shape=jax.ShapeDtypeStruct((s, h, dh), F32),
        compiler_params=_params("parallel", "arbitrary"),
        name="sb_sample",
    )(page_table.reshape(-1), q.reshape(s, h, dh), bias_lanes, u, *([k_t] * n_pp), *([v_t] * n_pp))
    return out.reshape(s, e)


def kernel(x_prompt, x_sample, c_prompt, c_sample, state_ssm_re, state_ssm_im, cache_k, cache_v, page_table, g_pre_a, g_post_a, w_mod_a, b_mod_a, w_in_a, lam_re, lam_im, log_dt, bmat_re, bmat_im, cmat_re, cmat_im, d_skip, w_glu, b_glu, w_out_a, g_kv, w_mod_kv, b_mod_kv, w_kv, g_pre_b, g_post_b, w_mod_b, b_mod_b, w_in_b, sb_bias, w_out_b):
    n_b, t, d = x_prompt.shape
    n_s = x_sample.shape[0]
    n_a, n_bl = w_in_a.shape[0], w_in_b.shape[0]
    e = d
    q_scale = HEAD_DIM ** -0.5
    bf = lambda w: w.astype(BF16)

    c_all = jnp.concatenate([c_prompt, c_sample], axis=0)

    def mods(w_layers, layer, b_mod):
        m = _ada_mod(c_all, w_layers, layer, b_mod)
        return m[:n_b].reshape(n_b, 1, -1), m[n_b:].reshape(1, n_s, -1)

    xp = x_prompt
    xs = x_sample.reshape(1, n_s, d)
    p_re, p_im, s_re, s_im = [], [], [], []
    uz_cfg = ((0, e, ((0, 1.0),)), (e, e, ((1, 1.0),)))
    for i in range(n_a):
        mp, ms = mods(w_mod_a, i, b_mod_a[i])
        w_in, w_g, w_o = bf(w_in_a[i]), bf(w_glu[i]), bf(w_out_a[i])
        prompt_ops, sample_ops = _s5_operators(
            _s5_prep(lam_re[i], lam_im[i], log_dt[i], bmat_re[i], bmat_im[i], cmat_re[i], cmat_im[i]))
        u, z = _norm_mm(xp, mp, g_pre_a[i], w_in, uz_cfg, (F32, BF16))
        y, hr, hi = _s5_scan(u, prompt_ops)
        xp = _s5_post(y, u, z, d_skip[i], w_g, b_glu[i], w_o, g_post_a[i], mp, xp)
        p_re.append(hr)
        p_im.append(hi)
        u, z = _norm_mm(xs, ms, g_pre_a[i], w_in, uz_cfg, (F32, F32))
        y, hr, hi = _s5_step(u[0], state_ssm_re[i], state_ssm_im[i], sample_ops)
        xs = _s5_post(y.reshape(1, n_s, e), u, z, d_skip[i], w_g, b_glu[i], w_o, g_post_a[i], ms, xs)
        s_re.append(hr)
        s_im.append(hi)

    mp, ms = mods(w_mod_kv[None], 0, b_mod_kv)
    w = bf(w_kv)
    kv_cfg = ((0, e, ((0, 1.0), (2, 1.0))), (e, e, ((1, 1.0), (3, 1.0))))
    p_k, p_v, pk_bf, pv_bf = _norm_mm(xp, mp, g_kv, w, kv_cfg, (F32, F32, BF16, BF16))
    kv_cfg_s = ((0, e, ((0, 1.0),)), (e, e, ((1, 1.0),)))
    s_k, s_v = _norm_mm(xs, ms, g_kv, w, kv_cfg_s, (F32, F32))

    qz_cfg = ((0, e, ((0, q_scale),)), (e, e, ((1, 1.0),)))
    qz_cfg_p = ((0, e, ((0, q_scale * LOG2E),)), (e, e, ((1, 1.0),)))
    for j in range(n_bl):
        mp, ms = mods(w_mod_b, j, b_mod_b[j])
        w_in, w_o = bf(w_in_b[j]), bf(w_out_b[j])
        q, z = _norm_mm(xp, mp, g_pre_b[j], w_in, qz_cfg_p, (BF16, BF16))
        a = _sb_prompt(q, pk_bf, pv_bf, sb_bias[j])
        xp = _sb_post(a, z, w_o, g_post_b[j], mp, xp)
        q, z = _norm_mm(xs, ms, g_pre_b[j], w_in, qz_cfg, (F32, F32))
        a = _sb_sample(q[0], cache_k, cache_v, page_table, sb_bias[j])
        xs = _sb_post(a.reshape(1, n_s, e), z, w_o, g_post_b[j], ms, xs)

    return (xp, xs.reshape(n_s, 1, d), jnp.stack(p_re), jnp.stack(p_im),
            p_k.reshape(n_b, t, N_HEADS, HEAD_DIM), p_v.reshape(n_b, t, N_HEADS, HEAD_DIM),
            jnp.stack(s_re), jnp.stack(s_im),
            s_k.reshape(n_s, 1, N_HEADS, HEAD_DIM), s_v.reshape(n_s, 1, N_HEADS, HEAD_DIM))
```

```python
import functools

import jax
import jax.numpy as jnp
from jax import lax
from jax.experimental import pallas as pl
from jax.experimental.pallas import tpu as pltpu

F32 = jnp.float32
BF16 = jnp.bfloat16

D_MODEL = 1024
GROUP_CH = 16
N_GROUPS = D_MODEL // GROUP_CH
P_STATE = 64
N_HEADS = 16
HEAD_DIM = D_MODEL // N_HEADS
EPS = 1e-6

SSM_CHUNK = 8
S5_TIME_TILE = 512
GROUPS_PER_OCTET = 8
LANES = 128
SUBLANES = 8
ROW_TILE = 512
ATTN_TILE = 256
ATTN_HEAD_PAIRS = 4
LOG2E = 1.4426950408889634
PAGES_PER_STEP = 8
VMEM_LIMIT = 48 * 1024 * 1024

_NT = (((1,), (1,)), ((), ()))


def _params(*sem):
    return pltpu.CompilerParams(dimension_semantics=sem, vmem_limit_bytes=VMEM_LIMIT)


def _sigmoid(x):
    return 0.5 * jnp.tanh(0.5 * x) + 0.5


def _softplus(z):
    return jnp.maximum(z, 0.0) + jnp.log(1.0 + jnp.exp(-jnp.abs(z)))


def _softplus2(z):
    return jnp.maximum(z, 0.0) + jnp.log2(1.0 + jnp.exp2(-jnp.abs(z)))


_GELU_C1 = 0.7978845608028654
_GELU_C3 = _GELU_C1 * 0.044715


def _gelu_tanh(x):
    half = 0.5 * x
    return half + half * jnp.tanh(x * (_GELU_C1 + _GELU_C3 * (x * x)))


def _split_bf16(x):
    hi = x.astype(BF16)
    lo = (x - hi.astype(F32)).astype(BF16)
    return hi, lo


def _dot(a, b):
    return jnp.dot(a, b, preferred_element_type=F32)


def _dot3(a, b, dims=None):
    a_hi, a_lo = _split_bf16(a)
    b_hi, b_lo = _split_bf16(b)
    if dims is None:
        f = _dot
    else:
        f = lambda p, q: lax.dot_general(p, q, dims, preferred_element_type=F32)
    return f(a_hi, b_hi) + (f(a_hi, b_lo) + f(a_lo, b_hi))


def _rms(x, g):
    return x * lax.rsqrt(jnp.mean(x * x, axis=-1, keepdims=True) + EPS) * g


def _mod_kernel(c_ref, w_ref, b_ref, o_ref):
    c = c_ref[...]
    s = (c * _sigmoid(c)).astype(BF16)
    o_ref[...] = _dot(s, w_ref[0].astype(BF16)) + b_ref[...]


def _ada_mod(c, w_layers, layer, b):
    m, d = c.shape
    n = w_layers.shape[2]
    tn = 1024
    return pl.pallas_call(
        _mod_kernel,
        grid=(n // tn,),
        in_specs=[pl.BlockSpec((m, d), lambda j: (0, 0)),
                  pl.BlockSpec((1, d, tn), lambda j: (layer, 0, j)),
                  pl.BlockSpec((1, tn), lambda j: (0, j))],
        out_specs=pl.BlockSpec((m, tn), lambda j: (0, j)),
        out_shape=jax.ShapeDtypeStruct((m, n), F32),
        compiler_params=_params("parallel"),
        name="ada_mod",
    )(c, w_layers, b.reshape(1, n))


def _norm_mm_kernel(x_ref, sh_ref, sc_ref, g_ref, w_ref, *out_refs, cfg, nchunk):
    h = _rms(x_ref[0], g_ref[...]) * (1.0 + sc_ref[0]) + sh_ref[0]
    hb = h.astype(BF16)
    for col0, ncols, outs in cfg:
        for j in range(0, ncols, nchunk):
            r = _dot(hb, w_ref[:, col0 + j:col0 + j + nchunk])
            for oi, scale in outs:
                v = r if scale == 1.0 else r * scale
                out_refs[oi][0, :, j:j + nchunk] = v.astype(out_refs[oi].dtype)


def _norm_mm(x, m, g, w, cfg, out_dtypes):
    bx, t, d = x.shape
    tm = min(ROW_TILE, t)
    per_row = m.shape[1] != 1
    tmm = tm if per_row else 1
    mod_idx = (lambda col: (lambda b, i: (b, i, col))) if per_row else (lambda col: (lambda b, i: (b, 0, col)))
    widths = {}
    for _, ncols, outs in cfg:
        for oi, _ in outs:
            widths[oi] = ncols
    n_out = len(out_dtypes)
    return pl.pallas_call(
        functools.partial(_norm_mm_kernel, cfg=cfg, nchunk=512),
        grid=(bx, t // tm),
        in_specs=[pl.BlockSpec((1, tm, d), lambda b, i: (b, i, 0)),
                  pl.BlockSpec((1, tmm, d), mod_idx(0)),
                  pl.BlockSpec((1, tmm, d), mod_idx(1)),
                  pl.BlockSpec((1, d), lambda b, i: (0, 0)),
                  pl.BlockSpec(w.shape, lambda b, i: (0, 0))],
        out_specs=[pl.BlockSpec((1, tm, widths[oi]), lambda b, i: (b, i, 0)) for oi in range(n_out)],
        out_shape=[jax.ShapeDtypeStruct((bx, t, widths[oi]), out_dtypes[oi]) for oi in range(n_out)],
        compiler_params=_params("parallel", "parallel"),
        name="norm_proj",
    )(x, m, m, g.reshape(1, d), w)


def _residual_tail(y, w_out_ref, g_ref, gate_ref, x_ref, o_ref):
    o = _dot(y.astype(BF16), w_out_ref[...])
    o_ref[0] = x_ref[0] + gate_ref[0] * _rms(o, g_ref[...])


def _s5_post_kernel(y_ref, u_ref, z_ref, dsk_ref, wglu_ref, bglu_ref, wout_ref, g_ref, gate_ref,
                    x_ref, o_ref):
    y = _gelu_tanh(y_ref[0] + dsk_ref[...] * u_ref[0])
    y = y * _sigmoid(_dot(y.astype(BF16), wglu_ref[...]) + bglu_ref[...])
    z = z_ref[0].astype(F32)
    y = y * (z * _sigmoid(z))
    _residual_tail(y, wout_ref, g_ref, gate_ref, x_ref, o_ref)


def _sb_post_kernel(a_ref, z_ref, wout_ref, g_ref, gate_ref, x_ref, o_ref):
    z = z_ref[0].astype(F32)
    y = a_ref[0].astype(F32) * (z * _sigmoid(z))
    _residual_tail(y, wout_ref, g_ref, gate_ref, x_ref, o_ref)


def _row_specs(x, m):
    bx, t, d = x.shape
    tm = min(ROW_TILE, t)
    per_row = m.shape[1] != 1
    tmm = tm if per_row else 1
    gate_idx = (lambda b, i: (b, i, 2)) if per_row else (lambda b, i: (b, 0, 2))
    row = lambda col: pl.BlockSpec((1, tm, d), lambda b, i: (b, i, col))
    vec = pl.BlockSpec((1, d), lambda b, i: (0, 0))
    mat = pl.BlockSpec((d, d), lambda b, i: (0, 0))
    gate = pl.BlockSpec((1, tmm, d), gate_idx)
    return (bx, t // tm), row, vec, mat, gate


def _s5_post(y, u, z, d_skip, w_glu, b_glu, w_out, g_post, m, x):
    grid, row, vec, mat, gate = _row_specs(x, m)
    d = x.shape[-1]
    return pl.pallas_call(
        _s5_post_kernel,
        grid=grid,
        in_specs=[row(0), row(0), row(0), vec, mat, vec, mat, vec, gate, row(0)],
        out_specs=row(0),
        out_shape=jax.ShapeDtypeStruct(x.shape, F32),
        compiler_params=_params("parallel", "parallel"),
        name="s5_post",
    )(y, u, z, d_skip.reshape(1, d), w_glu, b_glu.reshape(1, d), w_out, g_post.reshape(1, d), m, x)


def _sb_post(a, z, w_out, g_post, m, x):
    grid, row, vec, mat, gate = _row_specs(x, m)
    d = x.shape[-1]
    return pl.pallas_call(
        _sb_post_kernel,
        grid=grid,
        in_specs=[row(0), row(0), mat, vec, gate, row(0)],
        out_specs=row(0),
        out_shape=jax.ShapeDtypeStruct(x.shape, F32),
        compiler_params=_params("parallel", "parallel"),
        name="sb_post",
    )(a, z, w_out, g_post.reshape(1, d), m, x)


def _s5_prep_kernel(lre_ref, lim_ref, ldt_ref, btr_ref, bti_ref, cre_ref, cim_ref,
                    a_ref, al_ref, bbr_ref, bbi_ref, wsr_ref, wsi_ref, catr_ref, catn_ref, kft_ref):
    L = SSM_CHUNK
    for gi in range(lre_ref.shape[0]):
        lr, li = lre_ref[gi], lim_ref[gi]
        dt = jnp.exp(ldt_ref[gi])
        mag = jnp.exp(lr * dt)
        ar, ai = mag * jnp.cos(li * dt), mag * jnp.sin(li * dt)
        den = lr * lr + li * li
        cr = ((ar - 1.0) * lr + ai * li) / den
        ci = (ai * lr - (ar - 1.0) * li) / den
        btr, bti = btr_ref[gi], bti_ref[gi]
        bbr, bbi = cr * btr - ci * bti, cr * bti + ci * btr
        cre, cim = cre_ref[gi], cim_ref[gi]
        a_ref[gi, 0:1, :] = ar
        a_ref[gi, 1:2, :] = ai
        bbr_ref[gi] = bbr
        bbi_ref[gi] = bbi
        pr, pi = jnp.ones_like(ar), jnp.zeros_like(ar)
        for t in range(L + 1):
            rows = slice(t * GROUP_CH, (t + 1) * GROUP_CH)
            catr_ref[gi, rows, :] = cre * pr - cim * pi
            catn_ref[gi, rows, :] = -(cre * pi + cim * pr)
            if t < L:
                wrows = slice((L - 1 - t) * GROUP_CH, (L - t) * GROUP_CH)
                wsr_ref[gi, wrows, :] = pr * bbr - pi * bbi
                wsi_ref[gi, wrows, :] = pr * bbi + pi * bbr
                pr, pi = pr * ar - pi * ai, pr * ai + pi * ar
        al_ref[gi, 0:1, :] = pr
        al_ref[gi, 1:2, :] = pi
        kft_ref[gi] = _dot3(catr_ref[gi], bbr, _NT) + _dot3(catn_ref[gi], bbi, _NT)


def _s5_prep(lam_re, lam_im, log_dt, bmat_re, bmat_im, cmat_re, cmat_im):
    g, p, gc, L = N_GROUPS, P_STATE, GROUP_CH, SSM_CHUNK
    gs = GROUPS_PER_OCTET
    vec = pl.BlockSpec((gs, 1, p), lambda i: (i, 0, 0))
    mat = pl.BlockSpec((gs, gc, p), lambda i: (i, 0, 0))
    blk = lambda r, c: pl.BlockSpec((gs, r, c), lambda i: (i, 0, 0))
    shp = lambda r, c: jax.ShapeDtypeStruct((g, r, c), F32)
    return pl.pallas_call(
        _s5_prep_kernel,
        grid=(g // gs,),
        in_specs=[vec, vec, pl.BlockSpec((gs, 1, 1), lambda i: (i, 0, 0)), mat, mat, mat, mat],
        out_specs=[blk(2, p), blk(2, p), blk(gc, p), blk(gc, p), blk(L * gc, p), blk(L * gc, p),
                   blk((L + 1) * gc, p), blk((L + 1) * gc, p), blk((L + 1) * gc, gc)],
        out_shape=[shp(2, p), shp(2, p), shp(gc, p), shp(gc, p), shp(L * gc, p), shp(L * gc, p),
                   shp((L + 1) * gc, p), shp((L + 1) * gc, p), shp((L + 1) * gc, gc)],
        compiler_params=_params("parallel"),
        name="s5_prep",
    )(lam_re.reshape(g, 1, p), lam_im.reshape(g, 1, p), log_dt.reshape(g, 1, 1),
      jnp.swapaxes(bmat_re, 1, 2), jnp.swapaxes(bmat_im, 1, 2), cmat_re, cmat_im)


def _s5_operators(prep):
    a, al, bbr, bbi, wsr, wsi, catr, catn, kft = prep
    g, p, gc, L = N_GROUPS, P_STATE, GROUP_CH, SSM_CHUNK
    no, go = g // GROUPS_PER_OCTET, GROUPS_PER_OCTET
    eye = jnp.eye(go, dtype=F32)

    def by_step(x):
        return x.reshape(no, go, L, gc, -1).transpose(0, 2, 1, 3, 4).reshape(no, L * go * gc, -1)

    def by_state(x):
        return x[:, gc:].reshape(no, go, L * gc, p).transpose(0, 1, 3, 2).reshape(no, go * p, L * gc)

    k_lag = kft[:, :L * gc].reshape(no, go, L, gc, gc).transpose(0, 2, 1, 4, 3).reshape(no, L * go * gc, gc)
    w_state = jnp.concatenate([by_step(wsr), by_step(wsi)], axis=-1)
    w_carry = jnp.concatenate([by_state(catr), by_state(catn)], axis=1)
    prompt_ops = (k_lag.astype(BF16), w_state.astype(BF16), w_carry.astype(BF16),
                  al[:, 0].reshape(1, g * p), al[:, 1].reshape(1, g * p))

    def in_bd(bt):
        return jnp.einsum('ogcp,gh->ogchp', bt.reshape(no, go, gc, p), eye).reshape(no, go * gc, go * p)

    def out_bd(ct):
        return jnp.einsum('ogcp,gh->ogphc', ct.reshape(no, go, gc, p), eye).reshape(no, go * p, go * gc)

    sample_ops = (a[:, 0].reshape(1, g * p), a[:, 1].reshape(1, g * p), in_bd(bbr), in_bd(bbi),
                  out_bd(catr[:, :gc]), out_bd(catn[:, :gc]))
    return prompt_ops, sample_ops


def _spread_blocks(compact, row_group, col_group, col_source):
    n_rows, n_src = compact.shape
    n_cols = col_group[0]
    src = lax.broadcasted_iota(jnp.int32, (n_src, n_cols), 0)
    col = lax.broadcasted_iota(jnp.int32, (n_src, n_cols), 1)
    copy = jnp.where(src == col_source(col), 1.0, 0.0).astype(BF16)
    r = lax.broadcasted_iota(jnp.int32, (n_rows, n_cols), 0)
    j = lax.broadcasted_iota(jnp.int32, (n_rows, n_cols), 1)
    same = row_group(r) == col_group[1](j)
    return jnp.where(same, _dot(compact, copy), 0.0).astype(BF16)


def _s5_scan_kernel(u_ref, klag_c, wst_c, wcar_c, alr_ref, ali_ref, y_ref, hr_ref, hi_ref,
                    lagged, chunk_in, s_re, s_im, hp_re, hp_im, from_state, st_re, st_im,
                    klag_ref, wst_ref, wcar_ref, *, n_b, tt):
    L, W = SSM_CHUNK, LANES
    nc = tt // L
    hw = st_re.shape[-1]

    @pl.when(pl.program_id(1) == 0)
    def _():
        st_re[...] = jnp.zeros_like(st_re)
        st_im[...] = jnp.zeros_like(st_im)
        by_channel = lambda r: (r >> 4) & 7
        klag_ref[...] = _spread_blocks(klag_c[0], by_channel, (W, lambda j: j >> 4), lambda j: j & 15)
        wst_ref[...] = _spread_blocks(wst_c[0], by_channel, (2 * hw, lambda j: (j >> 6) & 7),
                                      lambda j: ((j >> 9) << 6) | (j & 63))
        wcar_ref[...] = _spread_blocks(wcar_c[0], lambda r: (r >> 6) & 7, (L * W, lambda j: (j >> 4) & 7),
                                       lambda j: ((j >> 7) << 4) | (j & 15))

    sub = lax.broadcasted_iota(jnp.int32, (nc, L, W), 1)
    for b in range(n_b):
        x3 = u_ref[b].reshape(nc, L, W)
        for lag in range(L):
            shifted = x3 if lag == 0 else jnp.where(sub >= lag, pltpu.roll(x3, lag, axis=1), 0.0)
            lagged[:, lag * W:(lag + 1) * W] = shifted.reshape(tt, W).astype(BF16)
        y_ref[b] = _dot(lagged[...], klag_ref[...])
        for l in range(L):
            chunk_in[l, pl.ds(b, nc, stride=n_b), :] = u_ref[b, pl.ds(l, nc, stride=L), :]

    s = _dot(jnp.concatenate([chunk_in[l].astype(BF16) for l in range(L)], axis=1), wst_ref[...])
    s_re[...] = s[:, :hw]
    s_im[...] = s[:, hw:]
    alr = jnp.broadcast_to(alr_ref[...], (n_b, hw))
    ali = jnp.broadcast_to(ali_ref[...], (n_b, hw))

    def step(k, carry):
        hr, hi = carry
        rows = pl.ds(pl.multiple_of(k * n_b, n_b), n_b)
        hp_re[rows, :] = hr
        hp_im[rows, :] = hi
        return (alr * hr - ali * hi + s_re[rows, :], alr * hi + ali * hr + s_im[rows, :])

    hr, hi = lax.fori_loop(0, nc, step, (st_re[...], st_im[...]))
    st_re[...] = hr
    st_im[...] = hi
    hr_ref[...] = hr
    hi_ref[...] = hi
    r = (_dot(hp_re[...].astype(BF16), wcar_ref[:hw, :])
         + _dot(hp_im[...].astype(BF16), wcar_ref[hw:, :]))
    for l in range(L):
        from_state[l] = r[:, l * W:(l + 1) * W]
    for b in range(n_b):
        for l in range(L):
            y_ref[b, pl.ds(l, nc, stride=L), :] += from_state[l, pl.ds(b, nc, stride=n_b), :]


def _s5_scan(u, ops):
    k_lag, w_state, w_carry, alr, ali = ops
    n_b, t, e = u.shape
    g, p, L, W = N_GROUPS, P_STATE, SSM_CHUNK, LANES
    no = e // W
    hw = GROUPS_PER_OCTET * p
    tt = min(S5_TIME_TILE, t)
    assert n_b == SUBLANES and L == SUBLANES and t % tt == 0 and GROUPS_PER_OCTET * GROUP_CH == W
    rows = n_b * tt // L
    per_octet = lambda r, c: pl.BlockSpec((1, r, c), lambda o, i: (o, 0, 0))
    state_vec = pl.BlockSpec((1, hw), lambda o, i: (0, o))
    state_out = pl.BlockSpec((n_b, hw), lambda o, i: (0, o))
    io = pl.BlockSpec((n_b, tt, W), lambda o, i: (0, i, o))
    y, hr, hi = pl.pallas_call(
        functools.partial(_s5_scan_kernel, n_b=n_b, tt=tt),
        grid=(no, t // tt),
        in_specs=[io, per_octet(L * W, GROUP_CH), per_octet(L * W, 2 * p), per_octet(2 * hw, L * GROUP_CH),
                  state_vec, state_vec],
        out_specs=[io, state_out, state_out],
        out_shape=[jax.ShapeDtypeStruct((n_b, t, e), F32),
                   jax.ShapeDtypeStruct((n_b, g * p), F32),
                   jax.ShapeDtypeStruct((n_b, g * p), F32)],
        scratch_shapes=[pltpu.VMEM((tt, L * W), BF16), pltpu.VMEM((L, rows, W), F32),
                        pltpu.VMEM((rows, hw), F32), pltpu.VMEM((rows, hw), F32),
                        pltpu.VMEM((rows, hw), F32), pltpu.VMEM((rows, hw), F32),
                        pltpu.VMEM((L, rows, W), F32),
                        pltpu.VMEM((n_b, hw), F32), pltpu.VMEM((n_b, hw), F32),
                        pltpu.VMEM((L * W, W), BF16), pltpu.VMEM((L * W, 2 * hw), BF16),
                        pltpu.VMEM((2 * hw, L * W), BF16)],
        compiler_params=_params("parallel", "arbitrary"),
        name="s5_scan",
    )(u, k_lag, w_state, w_carry, alr, ali)
    return y, hr.reshape(n_b, g, p), hi.reshape(n_b, g, p)


def _s5_step_kernel(u_ref, h0r_ref, h0i_ref, ar_ref, ai_ref, bbr_ref, bbi_ref, cr_ref, cn_ref,
                    y_ref, hr_ref, hi_ref):
    u = u_ref[...]
    ar, ai = ar_ref[...], ai_ref[...]
    h0r, h0i = h0r_ref[...], h0i_ref[...]
    hr = ar * h0r - ai * h0i + _dot3(u, bbr_ref[0])
    hi = ar * h0i + ai * h0r + _dot3(u, bbi_ref[0])
    hr_ref[...] = hr
    hi_ref[...] = hi
    y_ref[...] = _dot3(hr, cr_ref[0]) + _dot3(hi, cn_ref[0])


def _s5_step(u, h0_re, h0_im, ops):
    ar, ai, bbr, bbi, cr, cn = ops
    s, e = u.shape
    g, p = N_GROUPS, P_STATE
    no = g // GROUPS_PER_OCTET
    wc, ws = GROUPS_PER_OCTET * GROUP_CH, GROUPS_PER_OCTET * p
    assert wc == LANES
    ch = pl.BlockSpec((s, wc), lambda o: (0, o))
    st = pl.BlockSpec((s, ws), lambda o: (0, o))
    av = pl.BlockSpec((1, ws), lambda o: (0, o))
    w_in = pl.BlockSpec((1, wc, ws), lambda o: (o, 0, 0))
    w_out = pl.BlockSpec((1, ws, wc), lambda o: (o, 0, 0))
    y, hr, hi = pl.pallas_call(
        _s5_step_kernel,
        grid=(no,),
        in_specs=[ch, st, st, av, av, w_in, w_in, w_out, w_out],
        out_specs=[ch, st, st],
        out_shape=[jax.ShapeDtypeStruct((s, e), F32),
                   jax.ShapeDtypeStruct((s, g * p), F32),
                   jax.ShapeDtypeStruct((s, g * p), F32)],
        compiler_params=_params("parallel"),
        name="s5_step",
    )(u, h0_re.reshape(s, g * p), h0_im.reshape(s, g * p), ar, ai, bbr, bbi, cr, cn)
    return y, hr.reshape(s, g, p), hi.reshape(s, g, p)


def _sb_prompt_kernel(bias_ref, q_ref, k_ref, v_ref, u_ref, o_ref, acc, carry, w_s, z_s, *, tile, pairs):
    hg, qi = pl.program_id(1), pl.program_id(2)
    heads = range(2 * pairs)
    lanes = [slice((h // 2) * LANES, (h // 2 + 1) * LANES) for h in heads]
    first = lax.broadcasted_iota(jnp.int32, (tile, LANES), 1) < HEAD_DIM
    qm, bias = [], []
    for p in range(pairs):
        q2 = q_ref[0, :, p * LANES:(p + 1) * LANES]
        zero = jnp.zeros_like(q2)
        qm += [jnp.where(first, q2, zero), jnp.where(first, zero, q2)]
        bias += [bias_ref[2 * (hg * pairs + p)] * LOG2E, bias_ref[2 * (hg * pairs + p) + 1] * LOG2E]
    u = u_ref[...]
    row = lax.broadcasted_iota(jnp.int32, (tile, tile), 0)
    col = lax.broadcasted_iota(jnp.int32, (tile, tile), 1)
    below = col < row

    def block(j):
        return pl.ds(pl.multiple_of(j * tile, tile), tile)

    def weigh_values(j):
        for h in heads:
            acc[h] += _dot(w_s[h], v_ref[0, block(j), lanes[h]])

    def scores(j):
        return [lax.dot_general(qm[h], k_ref[0, block(j), lanes[h]], _NT, preferred_element_type=F32) + bias[h]
                for h in heads]

    def visit(j, diagonal):
        z = [z_s[h] for h in heads]
        z_next = scores(jnp.maximum(j - 1, 0))
        if not diagonal:
            weigh_values(j + 1)
        sp = [_softplus2(z[h]) for h in heads]
        if diagonal:
            sp = [jnp.where(below, sp[h], 0.0) for h in heads]
        cum = [_dot(sp[h].astype(BF16), u) for h in heads]
        total = [jnp.broadcast_to(cum[h][:, 0:1], (tile, LANES)) for h in heads]
        if diagonal:
            w = [jnp.where(below, jnp.exp2(z[h] + cum[h]), 0.0) for h in heads]
        else:
            w = [jnp.exp2(z[h] + (cum[h] + jnp.concatenate([carry[h]] * (tile // LANES), axis=1)))
                 for h in heads]
        for h in heads:
            w_s[h] = w[h].astype(BF16)
            z_s[h] = z_next[h]
            if diagonal:
                carry[h] = total[h]
            else:
                carry[h] += total[h]

    acc[...] = jnp.zeros_like(acc)
    for h, z0 in enumerate(scores(qi)):
        z_s[h] = z0
    visit(qi, True)

    def body(t, c):
        visit(qi - 1 - t, False)
        return c

    lax.fori_loop(0, qi, body, 0)
    weigh_values(0)
    for p in range(pairs):
        both = jnp.where(first, acc[2 * p], acc[2 * p + 1])
        o_ref[0, :, p * LANES:(p + 1) * LANES] = both.astype(o_ref.dtype)


def _sb_prompt(q, k, v, bias):
    n_b, t, e = q.shape
    tile = min(ATTN_TILE, t)
    pairs = ATTN_HEAD_PAIRS
    width = pairs * LANES
    assert 2 * HEAD_DIM == LANES and t % tile == 0 and tile % LANES == 0 and e % width == 0
    not_before = jnp.arange(tile)[:, None] >= jnp.arange(tile)[None, :]
    u = -not_before.astype(BF16)
    kv = pl.BlockSpec((1, t, width), lambda b, h, i: (b, 0, h))
    qo = pl.BlockSpec((1, tile, width), lambda b, h, i: (b, i, h))
    return pl.pallas_call(
        functools.partial(_sb_prompt_kernel, tile=tile, pairs=pairs),
        grid=(n_b, e // width, t // tile),
        in_specs=[pl.BlockSpec(memory_space=pltpu.SMEM), qo, kv, kv,
                  pl.BlockSpec((tile, tile), lambda b, h, i: (0, 0))],
        out_specs=qo,
        out_shape=jax.ShapeDtypeStruct((n_b, t, e), BF16),
        scratch_shapes=[pltpu.VMEM((2 * pairs, tile, LANES), F32),
                        pltpu.VMEM((2 * pairs, tile, LANES), F32),
                        pltpu.VMEM((2 * pairs, tile, tile), BF16),
                        pltpu.VMEM((2 * pairs, tile, tile), F32)],
        compiler_params=_params("parallel", "parallel", "parallel"),
        name="sb_prompt",
    )(bias, q, k, v, u)


def _sb_sample_kernel(pt_ref, q_ref, bias_ref, u_ref, *refs, n_pp):
    k_refs, v_refs = refs[:n_pp], refs[n_pp:2 * n_pp]
    o_ref, carry, acc, q_lanes = refs[2 * n_pp:]
    step = pl.program_id(1)
    n_h, dh, page = q_lanes.shape

    @pl.when(step == 0)
    def _():
        carry[...] = jnp.zeros_like(carry)
        acc[...] = jnp.zeros_like(acc)
        on_diag = (lax.broadcasted_iota(jnp.int32, (dh, dh), 0) == lax.broadcasted_iota(jnp.int32, (dh, dh), 1))
        ones = jnp.ones((dh, page), BF16)
        for hh in range(n_h):
            hi, lo = _split_bf16(jnp.where(on_diag, jnp.broadcast_to(q_ref[0, hh:hh + 1, :], (dh, dh)), 0.0))
            q_lanes[hh] = _dot(hi, ones) + _dot(lo, ones)

    q = q_lanes[...]
    u = u_ref[...]
    pages = range(n_pp)
    z = [jnp.sum(k_refs[i][0] * q, axis=1) + bias_ref[...] for i in pages]
    sp = [_softplus(z[i]) for i in pages]
    parts = [_split_bf16(sp[i]) for i in pages]
    cum = [_dot(parts[i][0], u) + _dot(parts[i][1], u) for i in pages]
    run = carry[...]
    w = []
    for i in pages:
        w.append(jnp.exp(z[i] - sp[i] + (cum[i] + run)))
        run = run + jnp.broadcast_to(cum[i][:, 0:1] - sp[i][:, 0:1], run.shape)
    carry[...] = run
    part = acc[...]
    for i in pages:
        part = part + v_refs[i][0] * w[i][:, None, :]
    acc[...] = part

    @pl.when(step == pl.num_programs(1) - 1)
    def _():
        o_ref[0] = jnp.sum(acc[...], axis=2)


def _sb_sample(q, cache_k, cache_v, page_table, bias):
    s, e = q.shape
    n_phys, page, h, dh = cache_k.shape
    n_pages = page_table.shape[1]
    n_pp = PAGES_PER_STEP if n_pages % PAGES_PER_STEP == 0 else 1
    assert page == LANES
    k_t = cache_k.transpose(0, 2, 3, 1)
    v_t = cache_v.transpose(0, 2, 3, 1)
    bias_lanes = jnp.broadcast_to(bias.reshape(h, 1), (h, page))
    t1 = jnp.arange(page)
    u = -(t1[:, None] > t1[None, :]).astype(BF16)

    def page_spec(i):
        def idx(b, g, pt):
            return (pt[b * n_pages + (n_pages - 1 - (g * n_pp + i))], 0, 0, 0)
        return pl.BlockSpec((1, h, dh, page), idx)

    pages = [page_spec(i) for i in range(n_pp)]
    const = lambda shape: pl.BlockSpec(shape, lambda b, g, pt: (0,) * len(shape))
    out = pl.pallas_call(
        functools.partial(_sb_sample_kernel, n_pp=n_pp),
        grid_spec=pltpu.PrefetchScalarGridSpec(
            num_scalar_prefetch=1,
            grid=(s, n_pages // n_pp),
            in_specs=[pl.BlockSpec((1, h, dh), lambda b, g, pt: (b, 0, 0)),
                      const((h, page)), const((page, page))] + pages + pages,
            out_specs=pl.BlockSpec((1, h, dh), lambda b, g, pt: (b, 0, 0)),
            scratch_shapes=[pltpu.VMEM((h, page), F32), pltpu.VMEM((h, dh, page), F32),
                            pltpu.VMEM((h, dh, page), F32)]),
        out_shape=jax.ShapeDtypeStruct((s, h, dh), F32),
        compiler_params=_params("parallel", "arbitrary"),
        name="sb_sample",
    )(page_table.reshape(-1), q.reshape(s, h, dh), bias_lanes, u, *([k_t] * n_pp), *([v_t] * n_pp))
    return out.reshape(s, e)


def kernel(x_prompt, x_sample, c_prompt, c_sample, state_ssm_re, state_ssm_im, cache_k, cache_v, page_table, g_pre_a, g_post_a, w_mod_a, b_mod_a, w_in_a, lam_re, lam_im, log_dt, bmat_re, bmat_im, cmat_re, cmat_im, d_skip, w_glu, b_glu, w_out_a, g_kv, w_mod_kv, b_mod_kv, w_kv, g_pre_b, g_post_b, w_mod_b, b_mod_b, w_in_b, sb_bias, w_out_b):
    n_b, t, d = x_prompt.shape
    n_s = x_sample.shape[0]
    n_a, n_bl = w_in_a.shape[0], w_in_b.shape[0]
    e = d
    q_scale = HEAD_DIM ** -0.5
    bf = lambda w: w.astype(BF16)

    c_all = jnp.concatenate([c_prompt, c_sample], axis=0)

    def mods(w_layers, layer, b_mod):
        m = _ada_mod(c_all, w_layers, layer, b_mod)
        return m[:n_b].reshape(n_b, 1, -1), m[n_b:].reshape(1, n_s, -1)

    xp = x_prompt
    xs = x_sample.reshape(1, n_s, d)
    p_re, p_im, s_re, s_im = [], [], [], []
    uz_cfg = ((0, e, ((0, 1.0),)), (e, e, ((1, 1.0),)))
    for i in range(n_a):
        mp, ms = mods(w_mod_a, i, b_mod_a[i])
        w_in, w_g, w_o = bf(w_in_a[i]), bf(w_glu[i]), bf(w_out_a[i])
        prompt_ops, sample_ops = _s5_operators(
            _s5_prep(lam_re[i], lam_im[i], log_dt[i], bmat_re[i], bmat_im[i], cmat_re[i], cmat_im[i]))
        u, z = _norm_mm(xp, mp, g_pre_a[i], w_in, uz_cfg, (F32, BF16))
        y, hr, hi = _s5_scan(u, prompt_ops)
        xp = _s5_post(y, u, z, d_skip[i], w_g, b_glu[i], w_o, g_post_a[i], mp, xp)
        p_re.append(hr)
        p_im.append(hi)
        u, z = _norm_mm(xs, ms, g_pre_a[i], w_in, uz_cfg, (F32, F32))
        y, hr, hi = _s5_step(u[0], state_ssm_re[i], state_ssm_im[i], sample_ops)
        xs = _s5_post(y.reshape(1, n_s, e), u, z, d_skip[i], w_g, b_glu[i], w_o, g_post_a[i], ms, xs)
        s_re.append(hr)
        s_im.append(hi)

    mp, ms = mods(w_mod_kv[None], 0, b_mod_kv)
    w = bf(w_kv)
    kv_cfg = ((0, e, ((0, 1.0), (2, 1.0))), (e, e, ((1, 1.0), (3, 1.0))))
    p_k, p_v, pk_bf, pv_bf = _norm_mm(xp, mp, g_kv, w, kv_cfg, (F32, F32, BF16, BF16))
    kv_cfg_s = ((0, e, ((0, 1.0),)), (e, e, ((1, 1.0),)))
    s_k, s_v = _norm_mm(xs, ms, g_kv, w, kv_cfg_s, (F32, F32))

    qz_cfg = ((0, e, ((0, q_scale),)), (e, e, ((1, 1.0),)))
    qz_cfg_p = ((0, e, ((0, q_scale * LOG2E),)), (e, e, ((1, 1.0),)))
    for j in range(n_bl):
        mp, ms = mods(w_mod_b, j, b_mod_b[j])
        w_in, w_o = bf(w_in_b[j]), bf(w_out_b[j])
        q, z = _norm_mm(xp, mp, g_pre_b[j], w_in, qz_cfg_p, (BF16, BF16))
        a = _sb_prompt(q, pk_bf, pv_bf, sb_bias[j])
        xp = _sb_post(a, z, w_o, g_post_b[j], mp, xp)
        q, z = _norm_mm(xs, ms, g_pre_b[j], w_in, qz_cfg, (F32, F32))
        a = _sb_sample(q[0], cache_k, cache_v, page_table, sb_bias[j])
        xs = _sb_post(a.reshape(1, n_s, e), z, w_o, g_post_b[j], ms, xs)

    return (xp, xs.reshape(n_s, 1, d), jnp.stack(p_re), jnp.stack(p_im),
            p_k.reshape(n_b, t, N_HEADS, HEAD_DIM), p_v.reshape(n_b, t, N_HEADS, HEAD_DIM),
            jnp.stack(s_re), jnp.stack(s_im),
            s_k.reshape(n_s, 1, N_HEADS, HEAD_DIM), s_v.reshape(n_s, 1, N_HEADS, HEAD_DIM))
```
